```python
import math
import jax
import jax.numpy as jnp
from jax import lax
import numpy as np

D_MODEL = 1024
BATCH = 8
SEQ = 2048
DEPTH = 2
DEC_BATCH = 128
DEC_SEQ = 4
PAST_LEN = 8192
PAGE_SIZE = 128

ML_HEADS = 4
ML_DK = 64
ML_DV = 128
ML_QK = ML_HEADS * ML_DK
ML_V = ML_HEADS * ML_DV
ML_CHUNK = 64
SB_HEADS = 8
SB_DH = 64
SB_W = SB_HEADS * SB_DH
SB_BLOCK = 128
SB_BIAS_LO = -9.0
SB_BIAS_HI = -5.0
S5_CH = D_MODEL // 2
S5_GROUP = 16
S5_GROUPS = S5_CH // S5_GROUP
S5_STATE = 64
S5_DT_MIN = 1e-3
S5_DT_MAX = 1e-1
SW_HEADS = 8
SW_KV_HEADS = 2
SW_DH = 64
SW_Q = SW_HEADS * SW_DH
SW_KV = SW_KV_HEADS * SW_DH
WINDOW = 128
D_FF = 4 * D_MODEL
EVEN_COLS = 2 * ML_QK + 2 * ML_V + 2 * ML_HEADS + 3 * SB_W
ODD_COLS = S5_CH + SW_Q + 2 * SW_KV
EPS = 1e-6

kernel_name = 'hybrid_mlstm_stickbreak_s5_swa_decode_step'


def rmsnorm(x, g):
    xf = x.astype(jnp.float32)
    y = xf * lax.rsqrt(jnp.mean(xf * xf, axis=-1, keepdims=True) + EPS)
    return (y * g.astype(jnp.float32)).astype(x.dtype)


def split_cols(a, sizes):
    out, off = [], 0
    for s in sizes:
        out.append(a[..., off:off + s])
        off += s
    return out


def sq_relu_mlp(h, w1, w2):
    return jnp.square(jax.nn.relu(h @ w1)) @ w2


def alibi_slopes(n):
    return jnp.asarray(2.0 ** (-8.0 * np.arange(1, n + 1) / n), dtype=jnp.float32)


def mlstm(q, k, v, i_pre, f_pre, C0, n0, m0):
    f32 = jnp.float32
    Bn, T = q.shape[0], q.shape[1]
    L = ML_CHUNK if T % ML_CHUNK == 0 else T
    nc = T // L

    def to_chunks(a):
        a = a.astype(f32).reshape((Bn, nc, L) + a.shape[2:])
        return jnp.moveaxis(a, 1, 0)

    causal = jnp.tril(jnp.ones((L, L), dtype=bool))
    xs = (to_chunks(q), to_chunks(k * (ML_DK ** -0.5)), to_chunks(v),
          to_chunks(i_pre), to_chunks(jax.nn.log_sigmoid(f_pre.astype(f32))))

    def step(carry, xc):
        C, n, m = carry
        qc, kc, vc, ic, lfc = xc
        b = jnp.moveaxis(jnp.cumsum(lfc, axis=1), 1, 2)
        it = jnp.moveaxis(ic, 1, 2)
        dmat = b[..., :, None] - b[..., None, :] + it[..., None, :]
        dmat = jnp.where(causal, dmat, -jnp.inf)
        inter = b + m[..., None]
        m_t = jnp.maximum(inter, jnp.max(dmat, axis=-1))
        w_inter = jnp.exp(inter - m_t)
        s = jnp.einsum('blhd,bshd->bhls', qc, kc) * jnp.exp(dmat - m_t[..., None])
        num = w_inter[..., None] * jnp.einsum('blhd,bhde->bhle', qc, C) + jnp.einsum('bhls,bshe->bhle', s, vc)
        den = w_inter * jnp.einsum('blhd,bhd->bhl', qc, n) + jnp.sum(s, axis=-1)
        h = num / jnp.maximum(jnp.abs(den), jnp.exp(-m_t))[..., None]
        m_new = m_t[..., -1]
        g_state = jnp.exp(inter[..., -1] - m_new)
        g_in = jnp.exp(b[..., -1:] - b + it - m_new[..., None])
        C_new = g_state[..., None, None] * C + jnp.einsum('bhs,bshd,bshe->bhde', g_in, kc, vc)
        n_new = g_state[..., None] * n + jnp.einsum('bhs,bshd->bhd', g_in, kc)
        return (C_new, n_new, m_new), jnp.moveaxis(h, 1, 2)

    carry0 = (C0.astype(f32), n0.astype(f32), m0.astype(f32))
    (C, n, m), hs = lax.scan(step, carry0, xs)
    h = jnp.moveaxis(hs, 0, 1).reshape(Bn, T, ML_HEADS, ML_DV)
    return h, (C, n, m)


def stick_break(z, mask, log_rem):
    log_keep = jnp.where(mask, jax.nn.log_sigmoid(-z), 0.0)
    later = lax.cumsum(log_keep, axis=z.ndim - 1, reverse=True) - log_keep
    w = jnp.where(mask, jnp.exp(jax.nn.log_sigmoid(z) + later + log_rem[..., None]), 0.0)
    return w, log_rem + jnp.sum(log_keep, axis=-1)


def sb_prompt(q, k, v, bias):
    Bn, S = q.shape[0], q.shape[1]
    nb = S // SB_BLOCK
    kpos = jnp.arange(S)
    qb = jnp.moveaxis(q.reshape(Bn, nb, SB_BLOCK, SB_HEADS, SB_DH), 1, 0)
    bh = bias.astype(jnp.float32)[None, :, None, None]

    def block(args):
        qi, start = args
        z = jnp.einsum('bqhd,bkhd->bhqk', qi, k).astype(jnp.float32) * (SB_DH ** -0.5) + bh
        tpos = start + jnp.arange(SB_BLOCK)
        mask = kpos[None, :] < tpos[:, None]
        w, _ = stick_break(z, mask, jnp.zeros(z.shape[:-1], jnp.float32))
        return jnp.einsum('bhqk,bkhd->bqhd', w, v.astype(jnp.float32))

    out = lax.map(block, (qb, jnp.arange(nb) * SB_BLOCK))
    return jnp.moveaxis(out, 0, 1).reshape(Bn, S, SB_W)


def sb_sample(q, k_new, v_new, bias, cache_k, cache_v, page_table):
    f32 = jnp.float32
    DB, T = q.shape[0], q.shape[1]
    scale = SB_DH ** -0.5
    bh = bias.astype(f32)[None, :, None, None]
    z = jnp.einsum('bqhd,bkhd->bhqk', q, k_new).astype(f32) * scale + bh
    strict = jnp.tril(jnp.ones((T, T), dtype=bool), k=-1)
    w, log_rem = stick_break(z, strict, jnp.zeros((DB, SB_HEADS, T), f32))
    acc = jnp.einsum('bhqk,bkhd->bhqd', w, v_new.astype(f32))
    full = jnp.ones((T, cache_k.shape[1]), dtype=bool)

    def page_step(carry, p_idx):
        acc, log_rem = carry
        phys = page_table[:, p_idx]
        kp = cache_k[phys]
        vp = cache_v[phys]
        zp = jnp.einsum('bqhd,bkhd->bhqk', q, kp).astype(f32) * scale + bh
        wp, log_rem = stick_break(zp, full, log_rem)
        acc = acc + jnp.einsum('bhqk,bkhd->bhqd', wp, vp.astype(f32))
        return (acc, log_rem), None

    n_pages = page_table.shape[1]
    (acc, _), _ = lax.scan(page_step, (acc, log_rem), jnp.arange(n_pages)[::-1])
    return jnp.moveaxis(acc, 1, 2).reshape(DB, T, SB_W)


def s5_mix(u, s_re0, s_im0, p):
    f32 = jnp.float32
    Bn, T = u.shape[0], u.shape[1]
    lre = jnp.minimum(p['s5_lambda_re'].astype(f32), -1e-4)
    lim = p['s5_lambda_im'].astype(f32)
    dt = jnp.exp(p['s5_log_dt'].astype(f32))[:, None]
    mag = jnp.exp(lre * dt)
    ab_re = mag * jnp.cos(lim * dt)
    ab_im = mag * jnp.sin(lim * dt)
    den = lre * lre + lim * lim
    c_re = ((ab_re - 1.0) * lre + ab_im * lim) / den
    c_im = (ab_im * lre - (ab_re - 1.0) * lim) / den
    B_re = p['s5_B_re'].astype(f32)
    B_im = p['s5_B_im'].astype(f32)
    bb_re = c_re[..., None] * B_re - c_im[..., None] * B_im
    bb_im = c_re[..., None] * B_im + c_im[..., None] * B_re
    ug = u.astype(f32).reshape(Bn, T, S5_GROUPS, S5_GROUP)
    bu_re = jnp.einsum('btgc,gpc->btgp', ug, bb_re)
    bu_im = jnp.einsum('btgc,gpc->btgp', ug, bb_im)
    a_re = jnp.broadcast_to(ab_re, bu_re.shape)
    a_im = jnp.broadcast_to(ab_im, bu_im.shape)

    def combine(e1, e2):
        a1r, a1i, b1r, b1i = e1
        a2r, a2i, b2r, b2i = e2
        return (a2r * a1r - a2i * a1i, a2r * a1i + a2i * a1r,
                a2r * b1r - a2i * b1i + b2r, a2r * b1i + a2i * b1r + b2i)

    Ar, Ai, Br, Bi = lax.associative_scan(combine, (a_re, a_im, bu_re, bu_im), axis=1)
    s0r = s_re0.astype(f32)[:, None]
    s0i = s_im0.astype(f32)[:, None]
    x_re = Ar * s0r - Ai * s0i + Br
    x_im = Ar * s0i + Ai * s0r + Bi
    y = (jnp.einsum('gcp,btgp->btgc', p['s5_C_re'].astype(f32), x_re)
         - jnp.einsum('gcp,btgp->btgc', p['s5_C_im'].astype(f32), x_im)
         + p['s5_D'].astype(f32).reshape(S5_GROUPS, S5_GROUP) * ug)
    y = jax.nn.gelu(y.reshape(Bn, T, S5_CH))
    y = y * jax.nn.sigmoid(y @ p['w_glu'].astype(f32) + p['b_glu'].astype(f32))
    return y, x_re[:, -1], x_im[:, -1]


def sink_softmax(z, mask, sink):
    z = jnp.where(mask, z, -jnp.inf)
    m = jnp.maximum(jnp.max(z, axis=-1, keepdims=True), sink)
    e = jnp.exp(z - m)
    return e / (jnp.sum(e, axis=-1, keepdims=True) + jnp.exp(sink - m))


def swa_prompt(q, k, v, sinks):
    f32 = jnp.float32
    Bn, S = q.shape[0], q.shape[1]
    nb = S // WINDOW
    G = SW_HEADS // SW_KV_HEADS
    qb = q.reshape(Bn, nb, WINDOW, SW_KV_HEADS, G, SW_DH)
    pad = jnp.zeros((Bn, WINDOW, SW_KV_HEADS, SW_DH), k.dtype)
    kp = jnp.concatenate([pad, k], axis=1).reshape(Bn, nb + 1, WINDOW, SW_KV_HEADS, SW_DH)
    vp = jnp.concatenate([pad, v], axis=1).reshape(Bn, nb + 1, WINDOW, SW_KV_HEADS, SW_DH)
    kband = jnp.concatenate([kp[:, :-1], kp[:, 1:]], axis=2)
    vband = jnp.concatenate([vp[:, :-1], vp[:, 1:]], axis=2)
    tpos = jnp.arange(nb)[:, None] * WINDOW + jnp.arange(WINDOW)[None, :]
    spos = jnp.arange(nb)[:, None] * WINDOW - WINDOW + jnp.arange(2 * WINDOW)[None, :]
    dist = tpos[:, :, None] - spos[:, None, :]
    mask = (dist >= 0) & (dist < WINDOW) & (spos[:, None, :] >= 0)
    slopes = alibi_slopes(SW_HEADS).reshape(SW_KV_HEADS, G)[None, None, :, :, None, None]
    z = jnp.einsum('bnqkgd,bnckd->bnkgqc', qb, kband).astype(f32) * (SW_DH ** -0.5)
    z = z - slopes * dist[None, :, None, None].astype(f32)
    sink = sinks.astype(f32).reshape(SW_KV_HEADS, G)[None, None, :, :, None, None]
    pr = sink_softmax(z, mask[None, :, None, None], sink)
    o = jnp.einsum('bnkgqc,bnckd->bnqkgd', pr, vband.astype(f32)).reshape(Bn, S, SW_Q)
    return o, k[:, -WINDOW:], v[:, -WINDOW:]


def swa_sample(q, k_new, v_new, buf_k, buf_v, sinks):
    f32 = jnp.float32
    DB, T = q.shape[0], q.shape[1]
    L = buf_k.shape[1]
    G = SW_HEADS // SW_KV_HEADS
    kk = jnp.concatenate([buf_k.astype(k_new.dtype), k_new], axis=1)
    vv = jnp.concatenate([buf_v.astype(v_new.dtype), v_new], axis=1)
    spos = PAST_LEN - L + jnp.arange(L + T)
    tpos = PAST_LEN + jnp.arange(T)
    dist = tpos[:, None] - spos[None, :]
    mask = (dist >= 0) & (dist < WINDOW)
    slopes = alibi_slopes(SW_HEADS).reshape(SW_KV_HEADS, G)[None, :, :, None, None]
    qg = q.reshape(DB, T, SW_KV_HEADS, G, SW_DH)
    z = jnp.einsum('bqkgd,bckd->bkgqc', qg, kk).astype(f32) * (SW_DH ** -0.5)
    z = z - slopes * dist[None, None, None].astype(f32)
    sink = sinks.astype(f32).reshape(SW_KV_HEADS, G)[None, :, :, None, None]
    pr = sink_softmax(z, mask[None, None, None], sink)
    o = jnp.einsum('bkgqc,bckd->bqkgd', pr, vv.astype(f32)).reshape(DB, T, SW_Q)
    return o, kk[:, -L:], vv[:, -L:]


def even_mixer(hn, C0, n0, m0, sb_fn, p):
    Bn, T = hn.shape[0], hn.shape[1]
    proj = hn @ p['w_in_even']
    q_ml, k_ml, v_ml, o_ml, i_ml, f_ml, q_sb, k_sb, v_sb = split_cols(
        proj, (ML_QK, ML_QK, ML_V, ML_V, ML_HEADS, ML_HEADS, SB_W, SB_W, SB_W))
    h_ml, (C, n, m) = mlstm(
        q_ml.reshape(Bn, T, ML_HEADS, ML_DK), k_ml.reshape(Bn, T, ML_HEADS, ML_DK),
        v_ml.reshape(Bn, T, ML_HEADS, ML_DV),
        i_ml.astype(jnp.float32) + p['b_igate'], f_ml.astype(jnp.float32) + p['b_fgate'], C0, n0, m0)
    h_ml = h_ml * lax.rsqrt(jnp.mean(h_ml * h_ml, axis=-1, keepdims=True) + EPS)
    h_ml = h_ml * p['g_mlstm_head'].astype(jnp.float32).reshape(ML_HEADS, ML_DV)
    h_ml = h_ml.reshape(Bn, T, ML_V) * jax.nn.sigmoid(o_ml.astype(jnp.float32))
    k_sb = k_sb.reshape(Bn, T, SB_HEADS, SB_DH)
    v_sb = v_sb.reshape(Bn, T, SB_HEADS, SB_DH)
    h_sb = sb_fn(q_sb.reshape(Bn, T, SB_HEADS, SB_DH), k_sb, v_sb, p['sb_bias'])
    mixed = jnp.concatenate([h_ml.astype(hn.dtype), h_sb.astype(hn.dtype)], axis=-1)
    return mixed @ p['w_out_even'], (C, n, m, k_sb, v_sb)


def odd_mixer(hn, s_re0, s_im0, swa_fn, p):
    Bn, T = hn.shape[0], hn.shape[1]
    proj = hn @ p['w_in_odd']
    u, q, k, v = split_cols(proj, (S5_CH, SW_Q, SW_KV, SW_KV))
    y_s5, s_re, s_im = s5_mix(u, s_re0, s_im0, p)
    o_sw, kbuf, vbuf = swa_fn(q.reshape(Bn, T, SW_HEADS, SW_DH), k.reshape(Bn, T, SW_KV_HEADS, SW_DH),
                              v.reshape(Bn, T, SW_KV_HEADS, SW_DH), p['swa_sinks'])
    mixed = jnp.concatenate([y_s5.astype(hn.dtype), o_sw.astype(hn.dtype)], axis=-1)
    return mixed @ p['w_out_odd'], (s_re, s_im, kbuf, vbuf)


def run_trunk(x, ml_state, sb_fn, s5_state, swa_fn, p):
    new_state = ()
    for layer in range(DEPTH):
        hn = rmsnorm(x, p['g_norm_mix'][layer])
        if layer % 2 == 0:
            y, st = even_mixer(hn, ml_state[0], ml_state[1], ml_state[2], sb_fn, p)
        else:
            y, st = odd_mixer(hn, s5_state[0], s5_state[1], swa_fn, p)
        x = x + y.astype(x.dtype)
        new_state = new_state + st
        hn = rmsnorm(x, p['g_norm_ffn'][layer])
        x = x + sq_relu_mlp(hn, p['w_ff1'][layer], p['w_ff2'][layer]).astype(x.dtype)
    return rmsnorm(x, p['g_norm_final']), new_state


def setup_inputs(seed: int = 0) -> dict:
    f32 = jnp.float32
    key = jax.random.key(seed)
    ks = list(jax.random.split(key, 48))

    def nrm(shape, scale):
        return jax.random.normal(ks.pop(), shape, f32) * scale

    n_pages = PAST_LEN // PAGE_SIZE
    used = DEC_BATCH * n_pages
    n_pool = used + max(1, used // 4)
    page_table = jax.random.permutation(ks.pop(), n_pool)[:used].reshape(DEC_BATCH, n_pages).astype(jnp.int32)
    G, P = S5_GROUPS, S5_STATE
    buf = min(WINDOW, PAST_LEN)
    inp = {}
    inp['x_prompt'] = nrm((BATCH, SEQ, D_MODEL), 1.0)
    inp['x_sample'] = nrm((DEC_BATCH, DEC_SEQ, D_MODEL), 1.0)
    inp['state_mlstm_C'] = nrm((DEC_BATCH, ML_HEADS, ML_DK, ML_DV), 0.3)
    inp['state_mlstm_n'] = nrm((DEC_BATCH, ML_HEADS, ML_DK), 0.3)
    inp['state_mlstm_m'] = nrm((DEC_BATCH, ML_HEADS), 1.0)
    inp['cache_sb_k'] = nrm((n_pool, PAGE_SIZE, SB_HEADS, SB_DH), 1.0)
    inp['cache_sb_v'] = nrm((n_pool, PAGE_SIZE, SB_HEADS, SB_DH), 1.0)
    inp['page_table'] = page_table
    inp['state_s5_re'] = nrm((DEC_BATCH, G, P), 0.3)
    inp['state_s5_im'] = nrm((DEC_BATCH, G, P), 0.3)
    inp['cache_swa_k'] = nrm((DEC_BATCH, buf, SW_KV_HEADS, SW_DH), 1.0)
    inp['cache_swa_v'] = nrm((DEC_BATCH, buf, SW_KV_HEADS, SW_DH), 1.0)
    inp['g_norm_mix'] = 1.0 + nrm((DEPTH, D_MODEL), 0.05)
    inp['g_norm_ffn'] = 1.0 + nrm((DEPTH, D_MODEL), 0.05)
    inp['g_norm_final'] = 1.0 + nrm((D_MODEL,), 0.05)
    inp['w_in_even'] = nrm((D_MODEL, EVEN_COLS), D_MODEL ** -0.5)
    inp['b_igate'] = nrm((ML_HEADS,), 0.1)
    inp['b_fgate'] = jnp.linspace(3.0, 6.0, ML_HEADS, dtype=f32) + nrm((ML_HEADS,), 0.1)
    inp['g_mlstm_head'] = 1.0 + nrm((ML_V,), 0.05)
    inp['sb_bias'] = jnp.linspace(SB_BIAS_LO, SB_BIAS_HI, SB_HEADS, dtype=f32) + nrm((SB_HEADS,), 0.1)
    inp['w_out_even'] = nrm((ML_V + SB_W, D_MODEL), (ML_V + SB_W) ** -0.5)
    inp['w_in_odd'] = nrm((D_MODEL, ODD_COLS), D_MODEL ** -0.5)
    inp['s5_lambda_re'] = -0.5 + nrm((G, P), 0.01)
    inp['s5_lambda_im'] = math.pi * jnp.arange(P, dtype=f32)[None, :] + nrm((G, P), 0.01)
    inp['s5_log_dt'] = jax.random.uniform(ks.pop(), (G,), f32, math.log(S5_DT_MIN), math.log(S5_DT_MAX))
    inp['s5_B_re'] = nrm((G, P, S5_GROUP), (2.0 * S5_GROUP) ** -0.5)
    inp['s5_B_im'] = nrm((G, P, S5_GROUP), (2.0 * S5_GROUP) ** -0.5)
    inp['s5_C_re'] = nrm((G, S5_GROUP, P), (2.0 * P) ** -0.5)
    inp['s5_C_im'] = nrm((G, S5_GROUP, P), (2.0 * P) ** -0.5)
    inp['s5_D'] = nrm((S5_CH,), 1.0)
    inp['w_glu'] = nrm((S5_CH, S5_CH), S5_CH ** -0.5)
    inp['b_glu'] = nrm((S5_CH,), 0.02)
    inp['swa_sinks'] = nrm((SW_HEADS,), 0.5)
    inp['w_out_odd'] = nrm((S5_CH + SW_Q, D_MODEL), (S5_CH + SW_Q) ** -0.5)
    inp['w_ff1'] = nrm((DEPTH, D_MODEL, D_FF), D_MODEL ** -0.5)
    inp['w_ff2'] = nrm((DEPTH, D_FF, D_MODEL), D_FF ** -0.5)
    return inp


def reference(x_prompt, x_sample, state_mlstm_C, state_mlstm_n, state_mlstm_m, cache_sb_k, cache_sb_v,
              page_table, state_s5_re, state_s5_im, cache_swa_k, cache_swa_v, g_norm_mix, g_norm_ffn,
              g_norm_final, w_in_even, b_igate, b_fgate, g_mlstm_head, sb_bias, w_out_even, w_in_odd,
              s5_lambda_re, s5_lambda_im, s5_log_dt, s5_B_re, s5_B_im, s5_C_re, s5_C_im, s5_D, w_glu, b_glu,
              swa_sinks, w_out_odd, w_ff1, w_ff2):
    f32 = jnp.float32
    p = {'g_norm_mix': g_norm_mix, 'g_norm_ffn': g_norm_ffn, 'g_norm_final': g_norm_final,
         'w_in_even': w_in_even, 'b_igate': b_igate, 'b_fgate': b_fgate, 'g_mlstm_head': g_mlstm_head,
         'sb_bias': sb_bias, 'w_out_even': w_out_even, 'w_in_odd': w_in_odd, 's5_lambda_re': s5_lambda_re,
         's5_lambda_im': s5_lambda_im, 's5_log_dt': s5_log_dt, 's5_B_re': s5_B_re, 's5_B_im': s5_B_im,
         's5_C_re': s5_C_re, 's5_C_im': s5_C_im, 's5_D': s5_D, 'w_glu': w_glu, 'b_glu': b_glu,
         'swa_sinks': swa_sinks, 'w_out_odd': w_out_odd, 'w_ff1': w_ff1, 'w_ff2': w_ff2}
    Bp = x_prompt.shape[0]
    ml0 = (jnp.zeros((Bp, ML_HEADS, ML_DK, ML_DV), f32), jnp.zeros((Bp, ML_HEADS, ML_DK), f32),
           jnp.zeros((Bp, ML_HEADS), f32))
    s50 = (jnp.zeros((Bp, S5_GROUPS, S5_STATE), f32), jnp.zeros((Bp, S5_GROUPS, S5_STATE), f32))
    y_prompt, (ml_C_p, ml_n_p, ml_m_p, sb_k_p, sb_v_p, s5_re_p, s5_im_p, swa_k_p, swa_v_p) = run_trunk(
        x_prompt, ml0, sb_prompt, s50, swa_prompt, p)
    sb_fn = lambda q, k, v, b: sb_sample(q, k, v, b, cache_sb_k, cache_sb_v, page_table)
    swa_fn = lambda q, k, v, s: swa_sample(q, k, v, cache_swa_k, cache_swa_v, s)
    y_sample, (ml_C_s, ml_n_s, ml_m_s, sb_k_s, sb_v_s, s5_re_s, s5_im_s, swa_k_s, swa_v_s) = run_trunk(
        x_sample, (state_mlstm_C, state_mlstm_n, state_mlstm_m), sb_fn, (state_s5_re, state_s5_im), swa_fn, p)
    return (y_prompt, y_sample, ml_C_p, ml_n_p, ml_m_p, sb_k_p, sb_v_p, s5_re_p, s5_im_p, swa_k_p, swa_v_p,
            ml_C_s, ml_n_s, ml_m_s, sb_k_s, sb_v_s, s5_re_s, s5_im_s, swa_k_s, swa_v_s)
```

```python
import functools
import math

import jax
import jax.numpy as jnp
import numpy as np
from jax import lax
from jax.experimental import pallas as pl
from jax.experimental.pallas import tpu as pltpu

F32 = jnp.float32
BF16 = jnp.bfloat16

D_MODEL = 1024
PAST_LEN = 8192
ML_HEADS, ML_DK, ML_DV = 4, 64, 128
ML_QK, ML_V = ML_HEADS * ML_DK, ML_HEADS * ML_DV
SB_HEADS, SB_DH = 8, 64
SB_W = SB_HEADS * SB_DH
S5_CH, S5_GROUP, S5_GROUPS, S5_STATE = 512, 16, 32, 64
S5_N = S5_GROUPS * S5_STATE
SW_HEADS, SW_KV_HEADS, SW_DH = 8, 2, 64
SW_Q, SW_KV = SW_HEADS * SW_DH, SW_KV_HEADS * SW_DH
SW_GROUP = SW_HEADS // SW_KV_HEADS
WINDOW = 128
D_FF = 4 * D_MODEL
EPS = 1e-6

LANES = 128
SUBLANES = 8
VMEM_LIMIT = 56 * 1024 * 1024
T_PAD = SUBLANES
ML_CHUNK = 128
SB_TILE = 128
PAGES_PER_STEP = 8
S5_TT = 32


def _params(sem):
    return pltpu.CompilerParams(dimension_semantics=sem, vmem_limit_bytes=VMEM_LIMIT)


def _dot(a, b):
    return jnp.dot(a, b, preferred_element_type=F32)


def _dot_nt(a, b):
    return lax.dot_general(a, b, (((1,), (1,)), ((), ())), preferred_element_type=F32)


def _dot_tn(a, b):
    return lax.dot_general(a, b, (((0,), (0,)), ((), ())), preferred_element_type=F32)


def _rms(x, g):
    ms = jnp.mean(x * x, axis=-1, keepdims=True)
    return x * lax.rsqrt(ms + EPS) * g


def _softplus(z):
    return jnp.maximum(z, 0.0) + jnp.log1p(jnp.exp(-jnp.abs(z)))


def _sigmoid(z):
    return 1.0 / (1.0 + jnp.exp(-z))


def _split_bf16(x):
    hi = x.astype(BF16)
    lo = (x - hi.astype(F32)).astype(BF16)
    return hi, lo


def _inproj_kernel(x_ref, g_ref, w_ref, o_ref, hn_ref):
    @pl.when(pl.program_id(1) == 0)
    def _():
        hn_ref[...] = _rms(x_ref[...], g_ref[...]).astype(BF16)

    o_ref[...] = _dot(hn_ref[...], w_ref[...])


def _inproj_gate_kernel(x_ref, g_ref, w_ref, wg_ref, gb_ref, o_ref, gate_ref, hn_ref):
    @pl.when(pl.program_id(1) == 0)
    def _():
        hn = _rms(x_ref[...], g_ref[...]).astype(BF16)
        hn_ref[...] = hn
        gate_ref[...] = _dot(hn, wg_ref[...]) + gb_ref[...]

    o_ref[...] = _dot(hn_ref[...], w_ref[...])


def _inproj(x, g, w, tn, wg=None, gb=None):
    M, D = x.shape
    N = w.shape[1]
    tm = min(512, M)
    grid = (M // tm, N // tn)
    x_spec = pl.BlockSpec((tm, D), lambda i, j: (i, 0))
    g_spec = pl.BlockSpec((1, D), lambda i, j: (0, 0))
    w_spec = pl.BlockSpec((D, tn), lambda i, j: (0, j))
    o_spec = pl.BlockSpec((tm, tn), lambda i, j: (i, j))
    scratch = [pltpu.VMEM((tm, D), BF16)]
    if wg is None:
        return pl.pallas_call(
            _inproj_kernel, grid=grid, in_specs=[x_spec, g_spec, w_spec], out_specs=o_spec,
            out_shape=jax.ShapeDtypeStruct((M, N), F32), scratch_shapes=scratch,
            compiler_params=_params(("parallel", "arbitrary")), name="inproj")(x, g, w)
    gate_spec = pl.BlockSpec((tm, LANES), lambda i, j: (i, 0))
    return pl.pallas_call(
        _inproj_gate_kernel, grid=grid,
        in_specs=[x_spec, g_spec, w_spec, pl.BlockSpec((D, LANES), lambda i, j: (0, 0)),
                  pl.BlockSpec((1, LANES), lambda i, j: (0, 0))],
        out_specs=[o_spec, gate_spec],
        out_shape=[jax.ShapeDtypeStruct((M, N), F32), jax.ShapeDtypeStruct((M, LANES), F32)],
        scratch_shapes=scratch, compiler_params=_params(("parallel", "arbitrary")),
        name="inproj_gate")(x, g, w, wg, gb)


def _mlstm_kernel(q_ref, k_ref, v_ref, o_ref, gt_ref, gh_ref, c0_ref, n0_ref, m0_ref,
                  h_ref, c_out, n_out, m_out, caug_scr, m_scr, *, G, L, last, nc):
    c = pl.program_id(1)
    row = lax.broadcasted_iota(jnp.int32, (L, L), 0)
    col = lax.broadcasted_iota(jnp.int32, (L, L), 1)
    causal = col <= row
    eye = col == row
    e0 = (lax.broadcasted_iota(jnp.int32, (L, LANES), 1) == 0).astype(F32)
    r64 = lax.broadcasted_iota(jnp.int32, (ML_DK, ML_DK), 0)
    c64 = lax.broadcasted_iota(jnp.int32, (ML_DK, ML_DK), 1)
    eye64 = r64 == c64
    lane_dk = lax.broadcasted_iota(jnp.int32, (ML_DK, LANES), 1)
    lane_row = lax.broadcasted_iota(jnp.int32, (1, LANES), 1)
    rowid = lax.broadcasted_iota(jnp.int32, (L, 1), 0)

    def to_row(colvec):
        return jnp.sum(jnp.where(eye, colvec, 0.0), axis=0, keepdims=True)

    def seq_body(g, carry):
        r0 = g * L
        if not isinstance(r0, int):
            r0 = pl.multiple_of(r0, L)
        rows = pl.ds(r0, L)

        @pl.when(c == 0)
        def _():
            for h in range(ML_HEADS):
                nrow = n0_ref[g, pl.ds(h, 1), :]
                ncol = jnp.sum(jnp.where(eye64, nrow, 0.0), axis=1, keepdims=True)
                caug_scr[g, h] = jnp.concatenate(
                    [c0_ref[g, h], jnp.where(lane_dk == 0, ncol, 0.0)], axis=1)
                m_scr[g, h] = jnp.broadcast_to(m0_ref[g][:, h:h + 1], (SUBLANES, LANES))

        gates = gt_ref[rows, :]
        m_row = jnp.zeros((1, LANES), F32)
        for h in range(ML_HEADS):
            q = q_ref[rows, pl.ds(h * ML_DK, ML_DK)]
            kc = k_ref[rows, pl.ds(h * ML_DK, ML_DK)] * (ML_DK ** -0.5)
            v = v_ref[rows, pl.ds(h * ML_DV, ML_DV)]
            icol = gates[:, h:h + 1]
            fcol = gates[:, ML_HEADS + h:ML_HEADS + h + 1]
            caug = caug_scr[g, h]
            m_prev = m_scr[g, h][0:1, 0:1]

            lfcol = jnp.minimum(fcol, 0.0) - jnp.log1p(jnp.exp(-jnp.abs(fcol)))
            lfrow = to_row(lfcol)
            irow = to_row(icol)
            bcol = jnp.sum(jnp.where(causal, lfrow, 0.0), axis=1, keepdims=True)
            brow = to_row(bcol)
            dmat = jnp.where(causal, bcol - brow + irow, -jnp.inf)
            inter = bcol + m_prev
            mt = jnp.maximum(inter, jnp.max(dmat, axis=1, keepdims=True))
            w_inter = jnp.exp(inter - mt)
            qb = q.astype(BF16)
            s = _dot_nt(qb, kc.astype(BF16)) * jnp.exp(dmat - mt)
            vaug = jnp.concatenate([v, e0], axis=1).astype(BF16)
            tot = w_inter * _dot(qb, caug.astype(BF16)) + _dot(s.astype(BF16), vaug)
            num = tot[:, :ML_DV]
            den = tot[:, ML_DV:ML_DV + 1]
            hh = num / jnp.maximum(jnp.abs(den), jnp.exp(-mt))

            m_new = mt[last:last + 1, :]
            g_state = jnp.exp(inter[last:last + 1, :] - m_new)
            gin = jnp.exp(bcol[last:last + 1, :] - bcol + icol - m_new)
            if last < L - 1:
                gin = jnp.where(rowid <= last, gin, 0.0)
            caug_new = g_state * caug + _dot_tn((kc * gin).astype(BF16), vaug)
            caug_scr[g, h] = caug_new
            m_scr[g, h] = jnp.broadcast_to(m_new, (SUBLANES, LANES))
            m_row = m_row + jnp.where(lane_row == h, m_new, 0.0)

            hh = hh * lax.rsqrt(jnp.mean(hh * hh, axis=-1, keepdims=True) + EPS)
            hh = hh * gh_ref[:, pl.ds(h * ML_DV, ML_DV)]
            hh = hh * _sigmoid(o_ref[rows, pl.ds(h * ML_DV, ML_DV)])
            h_ref[rows, pl.ds(h * ML_DV, ML_DV)] = hh

            @pl.when(c == nc - 1)
            def _():
                c_out[g, h] = caug_new[:, :ML_DV]
                ncol = caug_new[:, ML_DV:ML_DV + 1]
                n_out[g, pl.ds(h, 1), :] = jnp.sum(jnp.where(eye64, ncol, 0.0), axis=0, keepdims=True)

        @pl.when(c == nc - 1)
        def _():
            m_out[g] = m_row

        return carry

    if G == 1:
        seq_body(0, 0)
    else:
        lax.fori_loop(0, G, seq_body, 0)


def _mlstm(pe, gates, g_head, c0, n0, m0, *, n_seq, seq_len, last):
    M = pe.shape[0]
    if seq_len >= ML_CHUNK:
        G, L = 1, ML_CHUNK
    else:
        L = seq_len
        G = ML_CHUNK // L
    nc = seq_len // L
    last_l = last - (nc - 1) * L
    grid = (n_seq // G, nc)
    rb = lambda b, c: b * nc + c
    R = G * L
    kernel = functools.partial(_mlstm_kernel, G=G, L=L, last=last_l, nc=nc)
    return pl.pallas_call(
        kernel, grid=grid,
        in_specs=[
            pl.BlockSpec((R, ML_QK), lambda b, c: (rb(b, c), 0)),
            pl.BlockSpec((R, ML_QK), lambda b, c: (rb(b, c), 1)),
            pl.BlockSpec((R, ML_V), lambda b, c: (rb(b, c), 1)),
            pl.BlockSpec((R, ML_V), lambda b, c: (rb(b, c), 2)),
            pl.BlockSpec((R, LANES), lambda b, c: (rb(b, c), 0)),
            pl.BlockSpec((1, ML_V), lambda b, c: (0, 0)),
            pl.BlockSpec((G, ML_HEADS, ML_DK, ML_DV), lambda b, c: (b, 0, 0, 0)),
            pl.BlockSpec((G, ML_HEADS, ML_DK), lambda b, c: (b, 0, 0)),
            pl.BlockSpec((G, 1, ML_HEADS), lambda b, c: (b, 0, 0)),
        ],
        out_specs=[
            pl.BlockSpec((R, ML_V), lambda b, c: (rb(b, c), 0)),
            pl.BlockSpec((G, ML_HEADS, ML_DK, ML_DV), lambda b, c: (b, 0, 0, 0)),
            pl.BlockSpec((G, ML_HEADS, ML_DK), lambda b, c: (b, 0, 0)),
            pl.BlockSpec((G, 1, LANES), lambda b, c: (b, 0, 0)),
        ],
        out_shape=[
            jax.ShapeDtypeStruct((M, ML_V), F32),
            jax.ShapeDtypeStruct((n_seq, ML_HEADS, ML_DK, ML_DV), F32),
            jax.ShapeDtypeStruct((n_seq, ML_HEADS, ML_DK), F32),
            jax.ShapeDtypeStruct((n_seq, 1, LANES), F32),
        ],
        scratch_shapes=[pltpu.VMEM((G, ML_HEADS, ML_DK, 2 * ML_DV), F32),
                        pltpu.VMEM((G, ML_HEADS, SUBLANES, LANES), F32)],
        compiler_params=_params(("parallel", "arbitrary")), name="mlstm",
    )(pe, pe, pe, pe, gates, g_head, c0, n0, m0.reshape(n_seq, 1, ML_HEADS))


def _suffix_matrix(n):
    r = lax.broadcasted_iota(jnp.int32, (n, n), 0)
    c = lax.broadcasted_iota(jnp.int32, (n, n), 1)
    return (r > c).astype(BF16)


def _sb_tile(z, valid, lrem, u_mat):
    sp = _softplus(z)
    lk = -sp if valid is None else jnp.where(valid, -sp, 0.0)
    hi, lo = _split_bf16(lk)
    later = _dot(hi, u_mat) + _dot(lo, u_mat)
    w = jnp.exp(z - sp + later + lrem)
    if valid is not None:
        w = jnp.where(valid, w, 0.0)
    return w, lrem + jnp.sum(lk, axis=1, keepdims=True)


def _sb_prompt_kernel(bias_ref, q_ref, k_ref, v_ref, o_ref, ks_ref, vs_ref, *, T):
    qi = pl.program_id(1)

    @pl.when(qi == 0)
    def _():
        for h in range(SB_HEADS):
            ks_ref[h] = k_ref[:, pl.ds(h * SB_DH, SB_DH)].astype(BF16)
            vs_ref[h] = v_ref[:, pl.ds(h * SB_DH, SB_DH)].astype(BF16)

    u_mat = _suffix_matrix(T)
    row = lax.broadcasted_iota(jnp.int32, (T, T), 0)
    col = lax.broadcasted_iota(jnp.int32, (T, T), 1)
    strict = col < row
    outs = []
    for h in range(SB_HEADS):
        qh = (q_ref[:, pl.ds(h * SB_DH, SB_DH)] * (SB_DH ** -0.5)).astype(BF16)
        bias = bias_ref[h]

        def tile(j, acc, lrem, valid):
            ks = pl.ds(pl.multiple_of(j * T, T), T)
            z = _dot_nt(qh, ks_ref[h, ks, :]) + bias
            w, lrem = _sb_tile(z, valid, lrem, u_mat)
            return acc + _dot(w.astype(BF16), vs_ref[h, ks, :]), lrem

        acc, lrem = tile(qi, jnp.zeros((T, SB_DH), F32), jnp.zeros((T, 1), F32), strict)

        def body(it, carry):
            return tile(qi - 1 - it, carry[0], carry[1], None)

        acc, lrem = lax.fori_loop(0, qi, body, (acc, lrem))
        outs.append(acc)
    o_ref[...] = jnp.concatenate(outs, axis=1)


def _sb_prompt(pe, sb_bias, *, n_seq, seq_len):
    M = pe.shape[0]
    T = SB_TILE
    nq = seq_len // T
    kernel = functools.partial(_sb_prompt_kernel, T=T)
    return pl.pallas_call(
        kernel, grid=(n_seq, nq),
        in_specs=[
            pl.BlockSpec(memory_space=pltpu.SMEM),
            pl.BlockSpec((T, SB_W), lambda b, i: (b * nq + i, 3)),
            pl.BlockSpec((seq_len, SB_W), lambda b, i: (b, 4)),
            pl.BlockSpec((seq_len, SB_W), lambda b, i: (b, 5)),
        ],
        out_specs=pl.BlockSpec((T, SB_W), lambda b, i: (b * nq + i, 0)),
        out_shape=jax.ShapeDtypeStruct((M, SB_W), F32),
        scratch_shapes=[pltpu.VMEM((SB_HEADS, seq_len, SB_DH), BF16),
                        pltpu.VMEM((SB_HEADS, seq_len, SB_DH), BF16)],
        compiler_params=_params(("parallel", "arbitrary")), name="sb_prompt",
    )(sb_bias, pe, pe, pe)


def _sb_sample_kernel(pt_ref, bias_ref, qbd_ref, kn_ref, vn_ref, *refs, P, T):
    k_refs, v_refs = refs[:P], refs[P:2 * P]
    o_ref, acc_ref, lrem_ref = refs[2 * P], refs[2 * P + 1], refs[2 * P + 2]
    s = pl.program_id(1)
    R = SB_HEADS * T
    u_mat = _suffix_matrix(LANES)
    qb = (qbd_ref[0] * (SB_DH ** -0.5)).astype(BF16)
    bias = bias_ref[...]

    @pl.when(s == 0)
    def _():
        pad = jnp.zeros((LANES - T, SB_W), F32)
        kn = jnp.concatenate([kn_ref[0], pad], axis=0).astype(BF16)
        vn = jnp.concatenate([vn_ref[0], pad], axis=0).astype(BF16)
        t_of_row = lax.broadcasted_iota(jnp.int32, (R, LANES), 0) % T
        colk = lax.broadcasted_iota(jnp.int32, (R, LANES), 1)
        z = _dot_nt(qb, kn) + bias
        w, lrem = _sb_tile(z, colk < t_of_row, jnp.zeros((R, 1), F32), u_mat)
        acc_ref[...] = _dot(w.astype(BF16), vn)
        lrem_ref[...] = jnp.broadcast_to(lrem, (R, LANES))

    acc = acc_ref[...]
    lrem = lrem_ref[:, 0:1]
    for i in range(P):
        z = _dot_nt(qb, k_refs[i][0].astype(BF16)) + bias
        w, lrem = _sb_tile(z, None, lrem, u_mat)
        acc = acc + _dot(w.astype(BF16), v_refs[i][0].astype(BF16))
    acc_ref[...] = acc
    lrem_ref[...] = jnp.broadcast_to(lrem, (R, LANES))

    @pl.when(s == pl.num_programs(1) - 1)
    def _():
        o_ref[0] = jnp.concatenate(
            [acc[h * T:(h + 1) * T, h * SB_DH:(h + 1) * SB_DH] for h in range(SB_HEADS)], axis=1)


def _sb_sample(qbd, bias_col, k_new, v_new, cache_k, cache_v, page_table):
    DB, R, _ = qbd.shape
    T = R // SB_HEADS
    n_pages = page_table.shape[1]
    P = PAGES_PER_STEP
    page = cache_k.shape[1]
    ck = cache_k.reshape(cache_k.shape[0], page, SB_W)
    cv = cache_v.reshape(cache_v.shape[0], page, SB_W)

    def page_spec(i):
        return pl.BlockSpec((1, page, SB_W), lambda b, s, pt: (pt[b, n_pages - 1 - (s * P + i)], 0, 0))

    grid_spec = pltpu.PrefetchScalarGridSpec(
        num_scalar_prefetch=1, grid=(DB, n_pages // P),
        in_specs=[pl.BlockSpec((R, 1), lambda b, s, pt: (0, 0)),
                  pl.BlockSpec((1, R, SB_W), lambda b, s, pt: (b, 0, 0)),
                  pl.BlockSpec((1, T, SB_W), lambda b, s, pt: (b, 0, 0)),
                  pl.BlockSpec((1, T, SB_W), lambda b, s, pt: (b, 0, 0))]
        + [page_spec(i) for i in range(P)] + [page_spec(i) for i in range(P)],
        out_specs=pl.BlockSpec((1, T, SB_W), lambda b, s, pt: (b, 0, 0)),
        scratch_shapes=[pltpu.VMEM((R, SB_W), F32), pltpu.VMEM((R, LANES), F32)])
    kernel = functools.partial(_sb_sample_kernel, P=P, T=T)
    return pl.pallas_call(
        kernel, grid_spec=grid_spec, out_shape=jax.ShapeDtypeStruct((DB, T, SB_W), F32),
        compiler_params=_params(("parallel", "arbitrary")), name="sb_sample",
    )(page_table, bias_col, qbd, k_new, v_new, *([ck] * P), *([cv] * P))


def _s5_param_kernel(lre_ref, lim_ref, ldt_ref, bre_ref, bim_ref, are_ref, aim_ref, bbre_ref, bbim_ref):
    lre = jnp.minimum(lre_ref[...], -1e-4)
    lim = lim_ref[...]
    dt = jnp.exp(ldt_ref[...])
    mag = jnp.exp(lre * dt)
    ab_re = mag * jnp.cos(lim * dt)
    ab_im = mag * jnp.sin(lim * dt)
    den = lre * lre + lim * lim
    c_re = ((ab_re - 1.0) * lre + ab_im * lim) / den
    c_im = (ab_im * lre - (ab_re - 1.0) * lim) / den
    b_re, b_im = bre_ref[...], bim_ref[...]
    are_ref[...] = ab_re
    aim_ref[...] = ab_im
    bbre_ref[...] = c_re * b_re - c_im * b_im
    bbim_ref[...] = c_re * b_im + c_im * b_re


def _s5_params(lam_re, lam_im, log_dt, b_re, b_im):
    N = S5_N
    col = lambda a: a.reshape(N, 1)
    ldt = jnp.repeat(log_dt, S5_STATE).reshape(N, 1)
    full = lambda shape: pl.BlockSpec(shape, lambda: (0,) * len(shape))
    are, aim, bbre, bbim = pl.pallas_call(
        _s5_param_kernel,
        in_specs=[full((N, 1))] * 3 + [full((N, S5_GROUP))] * 2,
        out_specs=[full((N, 1))] * 2 + [full((N, S5_GROUP))] * 2,
        out_shape=[jax.ShapeDtypeStruct((N, 1), F32)] * 2 + [jax.ShapeDtypeStruct((N, S5_GROUP), F32)] * 2,
        name="s5_params",
    )(col(lam_re), col(lam_im), ldt, b_re.reshape(N, S5_GROUP), b_im.reshape(N, S5_GROUP))
    return are.reshape(1, N), aim.reshape(1, N), bbre, bbim


def _block_diag(blocks):
    G, r, c = blocks.shape
    eye = jnp.eye(G, dtype=blocks.dtype)
    return (blocks[:, :, None, :] * eye[:, None, :, None]).reshape(G * r, G * c)


def _s5_kernel(u_ref, are_ref, aim_ref, bb_ref, cre_ref, cim_ref, d_ref, wg_ref, bg_ref, sre0_ref, sim0_ref,
               y_ref, sre_out, sim_out, bu_scr, xs_scr, st_scr, *, TT, t_last):
    tb = pl.program_id(1)
    N = S5_N
    CH = 512
    BS = SUBLANES

    @pl.when(tb == 0)
    def _():
        st_scr[:, 0:N] = sre0_ref[...]
        st_scr[:, N:2 * N] = sim0_ref[...]

    u = u_ref[...].reshape(TT * BS, S5_CH)
    bu_scr[...] = _dot(u.astype(BF16), bb_ref[...])

    for cidx in range(N // CH):
        lo = cidx * CH
        ar = jnp.broadcast_to(are_ref[:, lo:lo + CH], (BS, CH))
        ai = jnp.broadcast_to(aim_ref[:, lo:lo + CH], (BS, CH))

        def step(t, carry):
            xr, xi = carry
            r = pl.ds(pl.multiple_of(t * BS, BS), BS)
            br = bu_scr[r, lo:lo + CH]
            bi = bu_scr[r, N + lo:N + lo + CH]
            nr = ar * xr - ai * xi + br
            ni = ar * xi + ai * xr + bi
            xs_scr[r, lo:lo + CH] = nr
            xs_scr[r, N + lo:N + lo + CH] = ni
            return nr, ni

        xr, xi = lax.fori_loop(0, TT, step, (st_scr[:, lo:lo + CH], st_scr[:, N + lo:N + lo + CH]))
        st_scr[:, lo:lo + CH] = xr
        st_scr[:, N + lo:N + lo + CH] = xi

    @pl.when(tb == t_last // TT)
    def _():
        r = pl.ds((t_last % TT) * BS, BS)
        sre_out[...] = xs_scr[r, 0:N]
        sim_out[...] = xs_scr[r, N:2 * N]

    xs = xs_scr[...]
    y = (_dot(xs[:, 0:N].astype(BF16), cre_ref[...]) - _dot(xs[:, N:2 * N].astype(BF16), cim_ref[...])
         + d_ref[...] * u)
    y = jax.nn.gelu(y)
    y = y * _sigmoid(_dot(y.astype(BF16), wg_ref[...]) + bg_ref[...])
    y_ref[...] = y.reshape(TT, BS, S5_CH)


def _s5(u_t, are, aim, bb, cre, cim, d, wg, bg, sre0, sim0, *, t_last):
    T, B, _ = u_t.shape
    TT = min(S5_TT, T)
    N = S5_N
    const = lambda shape: pl.BlockSpec(shape, lambda b, t: (0,) * len(shape))
    kernel = functools.partial(_s5_kernel, TT=TT, t_last=t_last)
    return pl.pallas_call(
        kernel, grid=(B // SUBLANES, T // TT),
        in_specs=[pl.BlockSpec((TT, SUBLANES, S5_CH), lambda b, t: (t, b, 0)),
                  const((1, N)), const((1, N)), const((S5_CH, 2 * N)), const((N, S5_CH)), const((N, S5_CH)),
                  const((1, S5_CH)), const((S5_CH, S5_CH)), const((1, S5_CH)),
                  pl.BlockSpec((SUBLANES, N), lambda b, t: (b, 0)),
                  pl.BlockSpec((SUBLANES, N), lambda b, t: (b, 0))],
        out_specs=[pl.BlockSpec((TT, SUBLANES, S5_CH), lambda b, t: (t, b, 0)),
                   pl.BlockSpec((SUBLANES, N), lambda b, t: (b, 0)),
                   pl.BlockSpec((SUBLANES, N), lambda b, t: (b, 0))],
        out_shape=[jax.ShapeDtypeStruct((T, B, S5_CH), F32),
                   jax.ShapeDtypeStruct((B, N), F32), jax.ShapeDtypeStruct((B, N), F32)],
        scratch_shapes=[pltpu.VMEM((TT * SUBLANES, 2 * N), F32), pltpu.VMEM((TT * SUBLANES, 2 * N), F32),
                        pltpu.VMEM((SUBLANES, 2 * N), F32)],
        compiler_params=_params(("parallel", "arbitrary")), name="s5",
    )(u_t, are, aim, bb, cre, cim, d, wg, bg, sre0, sim0)


def _alibi_slope(h):
    return float(2.0 ** (-8.0 * (h + 1) / SW_HEADS))


def _sink_softmax_parts(zs, sink):
    m = sink
    for z in zs:
        m = jnp.maximum(m, jnp.max(z, axis=-1, keepdims=True))
    es = [jnp.exp(z - m) for z in zs]
    tot = jnp.exp(sink - m)
    for e in es:
        tot = tot + jnp.sum(e, axis=-1, keepdims=True)
    return [e / tot for e in es]


def _swa_prompt_kernel(sink_ref, q_ref, kp_ref, kc_ref, vp_ref, vc_ref, o_ref):
    i = pl.program_id(1)
    W = WINDOW
    rowq = lax.broadcasted_iota(jnp.int32, (W, 2 * W), 0)
    colc = lax.broadcasted_iota(jnp.int32, (W, 2 * W), 1)
    dist = W + rowq - colc
    valid = (dist >= 0) & (dist < W) & (colc >= jnp.where(i > 0, 0, W))
    distf = dist.astype(F32)
    outs = []
    for kv in range(SW_KV_HEADS):
        ls = pl.ds(kv * SW_DH, SW_DH)
        kband = jnp.concatenate([kp_ref[:, ls], kc_ref[:, ls]], axis=0).astype(BF16)
        vband = jnp.concatenate([vp_ref[:, ls], vc_ref[:, ls]], axis=0).astype(BF16)
        for gq in range(SW_GROUP):
            h = kv * SW_GROUP + gq
            qh = q_ref[:, pl.ds(h * SW_DH, SW_DH)].astype(BF16)
            z = _dot_nt(qh, kband) * (SW_DH ** -0.5) - _alibi_slope(h) * distf
            z = jnp.where(valid, z, -jnp.inf)
            (p,) = _sink_softmax_parts([z], sink_ref[h])
            outs.append(_dot(p.astype(BF16), vband))
    o_ref[...] = jnp.concatenate(outs, axis=1)


def _swa_prompt(po, sinks, *, n_seq, seq_len):
    M = po.shape[0]
    W = WINDOW
    nb = seq_len // W
    kcol, vcol = (S5_CH + SW_Q) // SW_KV, (S5_CH + SW_Q) // SW_KV + 1
    cur = lambda b, i: b * nb + i
    prev = lambda b, i: b * nb + jnp.maximum(i - 1, 0)
    return pl.pallas_call(
        _swa_prompt_kernel, grid=(n_seq, nb),
        in_specs=[pl.BlockSpec(memory_space=pltpu.SMEM),
                  pl.BlockSpec((W, SW_Q), lambda b, i: (cur(b, i), 1)),
                  pl.BlockSpec((W, SW_KV), lambda b, i: (prev(b, i), kcol)),
                  pl.BlockSpec((W, SW_KV), lambda b, i: (cur(b, i), kcol)),
                  pl.BlockSpec((W, SW_KV), lambda b, i: (prev(b, i), vcol)),
                  pl.BlockSpec((W, SW_KV), lambda b, i: (cur(b, i), vcol))],
        out_specs=pl.BlockSpec((W, SW_Q), lambda b, i: (cur(b, i), 0)),
        out_shape=jax.ShapeDtypeStruct((M, SW_Q), F32),
        compiler_params=_params(("parallel", "arbitrary")), name="swa_prompt",
    )(sinks, po, po, po, po, po)


def _swa_sample_kernel(sink_ref, q_ref, kn_ref, vn_ref, ck_ref, cv_ref, o_ref, *, G, T):
    W = WINDOW
    R = SW_GROUP * T
    t_c = lax.broadcasted_iota(jnp.int32, (R, W), 0) % T
    j_c = lax.broadcasted_iota(jnp.int32, (R, W), 1)
    dist_c = W + t_c - j_c
    valid_c = dist_c < W
    t_n = lax.broadcasted_iota(jnp.int32, (R, T), 0) % T
    s_n = lax.broadcasted_iota(jnp.int32, (R, T), 1)
    dist_n = t_n - s_n
    valid_n = dist_n >= 0
    g_of_row = lax.broadcasted_iota(jnp.int32, (R, 1), 0) // T

    def seq_body(g, carry):
        rows = pl.ds(pl.multiple_of(g * T, T), T)
        outs = []
        for kv in range(SW_KV_HEADS):
            ls = pl.ds(kv * SW_DH, SW_DH)
            slope = jnp.zeros((R, 1), F32)
            sink = jnp.zeros((R, 1), F32)
            for gq in range(SW_GROUP):
                h = kv * SW_GROUP + gq
                slope = jnp.where(g_of_row == gq, _alibi_slope(h), slope)
                sink = jnp.where(g_of_row == gq, sink_ref[h], sink)
            q4 = jnp.concatenate(
                [q_ref[rows, pl.ds((kv * SW_GROUP + gq) * SW_DH, SW_DH)] for gq in range(SW_GROUP)],
                axis=0).astype(BF16)
            kc = ck_ref[g, :, ls].astype(BF16)
            vc = cv_ref[g, :, ls].astype(BF16)
            kn = kn_ref[rows, ls].astype(BF16)
            vn = vn_ref[rows, ls].astype(BF16)
            zc = _dot_nt(q4, kc) * (SW_DH ** -0.5) - slope * dist_c.astype(F32)
            zn = _dot_nt(q4, kn) * (SW_DH ** -0.5) - slope * dist_n.astype(F32)
            zc = jnp.where(valid_c, zc, -jnp.inf)
            zn = jnp.where(valid_n, zn, -jnp.inf)
            pc, pn = _sink_softmax_parts([zc, zn], sink)
            o4 = _dot(pc.astype(BF16), vc) + _dot(pn.astype(BF16), vn)
            outs += [o4[gq * T:(gq + 1) * T] for gq in range(SW_GROUP)]
        o_ref[rows, :] = jnp.concatenate(outs, axis=1)
        return carry

    lax.fori_loop(0, G, seq_body, 0)


def _swa_sample(po, sinks, cache_k, cache_v, *, n_seq, T):
    M = po.shape[0]
    G = 16
    kcol, vcol = (S5_CH + SW_Q) // SW_KV, (S5_CH + SW_Q) // SW_KV + 1
    kernel = functools.partial(_swa_sample_kernel, G=G, T=T)
    return pl.pallas_call(
        kernel, grid=(n_seq // G,),
        in_specs=[pl.BlockSpec(memory_space=pltpu.SMEM),
                  pl.BlockSpec((G * T, SW_Q), lambda b: (b, 1)),
                  pl.BlockSpec((G * T, SW_KV), lambda b: (b, kcol)),
                  pl.BlockSpec((G * T, SW_KV), lambda b: (b, vcol)),
                  pl.BlockSpec((G, WINDOW, SW_KV), lambda b: (b, 0, 0)),
                  pl.BlockSpec((G, WINDOW, SW_KV), lambda b: (b, 0, 0))],
        out_specs=pl.BlockSpec((G * T, SW_Q), lambda b: (b, 0)),
        out_shape=jax.ShapeDtypeStruct((M, SW_Q), F32),
        compiler_params=_params(("parallel",)), name="swa_sample",
    )(sinks, po, po, po, cache_k, cache_v)


def _post_kernel(x_ref, a_ref, b_ref, wa_ref, wb_ref, g_ref, w1_ref, w2_ref, gf_ref, o_ref,
                 x1_scr, hn_scr, acc_scr, *, final):
    j = pl.program_id(1)

    @pl.when(j == 0)
    def _():
        x1 = x_ref[...] + _dot(a_ref[...].astype(BF16), wa_ref[...]) + _dot(b_ref[...].astype(BF16), wb_ref[...])
        x1_scr[...] = x1
        hn_scr[...] = _rms(x1, g_ref[...]).astype(BF16)
        acc_scr[...] = jnp.zeros_like(acc_scr)

    hmid = jnp.square(jnp.maximum(_dot(hn_scr[...], w1_ref[...]), 0.0))
    acc_scr[...] += _dot(hmid.astype(BF16), w2_ref[...])

    @pl.when(j == pl.num_programs(1) - 1)
    def _():
        y = x1_scr[...] + acc_scr[...]
        if final:
            y = _rms(y, gf_ref[...])
        o_ref[...] = y


def _post(x, a, b, wa, wb, g, w1, w2, gf, *, final):
    M, D = x.shape
    tm = min(512, M)
    tf = 1024
    Ka, Kb = a.shape[1], b.shape[1]
    kernel = functools.partial(_post_kernel, final=final)
    return pl.pallas_call(
        kernel, grid=(M // tm, D_FF // tf),
        in_specs=[pl.BlockSpec((tm, D), lambda i, j: (i, 0)),
                  pl.BlockSpec((tm, Ka), lambda i, j: (i, 0)),
                  pl.BlockSpec((tm, Kb), lambda i, j: (i, 0)),
                  pl.BlockSpec((Ka, D), lambda i, j: (0, 0)),
                  pl.BlockSpec((Kb, D), lambda i, j: (0, 0)),
                  pl.BlockSpec((1, D), lambda i, j: (0, 0)),
                  pl.BlockSpec((D, tf), lambda i, j: (0, j)),
                  pl.BlockSpec((tf, D), lambda i, j: (j, 0)),
                  pl.BlockSpec((1, D), lambda i, j: (0, 0))],
        out_specs=pl.BlockSpec((tm, D), lambda i, j: (i, 0)),
        out_shape=jax.ShapeDtypeStruct((M, D), F32),
        scratch_shapes=[pltpu.VMEM((tm, D), F32), pltpu.VMEM((tm, D), BF16), pltpu.VMEM((tm, D), F32)],
        compiler_params=_params(("parallel", "arbitrary")), name="post",
    )(x, a, b, wa, wb, g, w1, w2, gf)


def _trunk(x, ml_state, s5_state, sb_fn, swa_fn, p, *, n_seq, T, t_real):
    last = t_real - 1
    pe, gates = _inproj(x, p["g_mix0"], p["w_even"], 1024, p["w_gate"], p["b_gate"])
    h_ml, c_new, n_new, m_new = _mlstm(pe, gates, p["g_head"], *ml_state, n_seq=n_seq, seq_len=T, last=last)
    h_sb = sb_fn(pe)
    x = _post(x, h_ml, h_sb, p["wo_even_a"], p["wo_even_b"], p["g_ffn0"], p["w1_0"], p["w2_0"],
              p["g_final"], final=False)
    po = _inproj(x, p["g_mix1"], p["w_odd"], 640)
    u_t = po[:, :S5_CH].reshape(n_seq, T, S5_CH).transpose(1, 0, 2)
    y_t, s_re, s_im = _s5(u_t, p["a_re"], p["a_im"], p["bb"], p["cc_re"], p["cc_im"], p["s5_d"], p["w_glu"],
                          p["b_glu"], s5_state[0].reshape(n_seq, S5_N), s5_state[1].reshape(n_seq, S5_N),
                          t_last=last)
    y_s5 = y_t.transpose(1, 0, 2).reshape(n_seq * T, S5_CH)
    o_sw = swa_fn(po)
    x = _post(x, y_s5, o_sw, p["wo_odd_a"], p["wo_odd_b"], p["g_ffn1"], p["w1_1"], p["w2_1"],
              p["g_final"], final=True)
    state = (c_new, n_new, m_new[:, 0, :ML_HEADS], pe, s_re.reshape(n_seq, S5_GROUPS, S5_STATE),
             s_im.reshape(n_seq, S5_GROUPS, S5_STATE), po)
    return x, state


def kernel(x_prompt, x_sample, state_mlstm_C, state_mlstm_n, state_mlstm_m, cache_sb_k, cache_sb_v, page_table, state_s5_re, state_s5_im, cache_swa_k, cache_swa_v, g_norm_mix, g_norm_ffn, g_norm_final, w_in_even, b_igate, b_fgate, g_mlstm_head, sb_bias, w_out_even, w_in_odd, s5_lambda_re, s5_lambda_im, s5_log_dt, s5_B_re, s5_B_im, s5_C_re, s5_C_im, s5_D, w_glu, b_glu, swa_sinks, w_out_odd, w_ff1, w_ff2):
    Bp, S, D = x_prompt.shape
    DB, Ts, _ = x_sample.shape
    n_gate = 2 * ML_HEADS
    g0 = 2 * ML_QK + 2 * ML_V
    row = lambda a: a.reshape(1, -1).astype(F32)

    a_re, a_im, bb_re, bb_im = _s5_params(s5_lambda_re, s5_lambda_im, s5_log_dt, s5_B_re, s5_B_im)
    bbt = lambda bb: _block_diag(bb.reshape(S5_GROUPS, S5_STATE, S5_GROUP).transpose(0, 2, 1))
    p = {
        "g_mix0": row(g_norm_mix[0]), "g_mix1": row(g_norm_mix[1]),
        "g_ffn0": row(g_norm_ffn[0]), "g_ffn1": row(g_norm_ffn[1]), "g_final": row(g_norm_final),
        "w_even": jnp.concatenate([w_in_even[:, :g0], w_in_even[:, g0 + n_gate:]], axis=1).astype(BF16),
        "w_gate": jnp.pad(w_in_even[:, g0:g0 + n_gate], ((0, 0), (0, LANES - n_gate))).astype(BF16),
        "b_gate": jnp.pad(jnp.concatenate([b_igate, b_fgate]), (0, LANES - n_gate)).reshape(1, LANES).astype(F32),
        "g_head": row(g_mlstm_head),
        "wo_even_a": w_out_even[:ML_V].astype(BF16), "wo_even_b": w_out_even[ML_V:].astype(BF16),
        "w_odd": w_in_odd.astype(BF16),
        "a_re": a_re, "a_im": a_im,
        "bb": jnp.concatenate([bbt(bb_re), bbt(bb_im)], axis=1).astype(BF16),
        "cc_re": _block_diag(s5_C_re.transpose(0, 2, 1)).astype(BF16),
        "cc_im": _block_diag(s5_C_im.transpose(0, 2, 1)).astype(BF16),
        "s5_d": row(s5_D), "w_glu": w_glu.astype(BF16), "b_glu": row(b_glu),
        "wo_odd_a": w_out_odd[:S5_CH].astype(BF16), "wo_odd_b": w_out_odd[S5_CH:].astype(BF16),
        "w1_0": w_ff1[0].astype(BF16), "w2_0": w_ff2[0].astype(BF16),
        "w1_1": w_ff1[1].astype(BF16), "w2_1": w_ff2[1].astype(BF16),
    }
    sb_bias = sb_bias.astype(F32)
    sinks = swa_sinks.astype(F32)

    ml0 = (jnp.zeros((Bp, ML_HEADS, ML_DK, ML_DV), F32), jnp.zeros((Bp, ML_HEADS, ML_DK), F32),
           jnp.zeros((Bp, ML_HEADS), F32))
    s50 = (jnp.zeros((Bp, S5_GROUPS, S5_STATE), F32), jnp.zeros((Bp, S5_GROUPS, S5_STATE), F32))
    y_p, (c_p, n_p, m_p, pe_p, sre_p, sim_p, po_p) = _trunk(
        x_prompt.reshape(Bp * S, D), ml0, s50,
        functools.partial(_sb_prompt, sb_bias=sb_bias, n_seq=Bp, seq_len=S),
        functools.partial(_swa_prompt, sinks=sinks, n_seq=Bp, seq_len=S),
        p, n_seq=Bp, T=S, t_real=S)
    ksb0 = 3 * SB_W + SB_W
    sb_k_p = pe_p[:, ksb0:ksb0 + SB_W].reshape(Bp, S, SB_HEADS, SB_DH)
    sb_v_p = pe_p[:, ksb0 + SB_W:].reshape(Bp, S, SB_HEADS, SB_DH)
    po3 = po_p.reshape(Bp, S, -1)
    swa_k_p = po3[:, -WINDOW:, S5_CH + SW_Q:S5_CH + SW_Q + SW_KV].reshape(Bp, WINDOW, SW_KV_HEADS, SW_DH)
    swa_v_p = po3[:, -WINDOW:, S5_CH + SW_Q + SW_KV:].reshape(Bp, WINDOW, SW_KV_HEADS, SW_DH)

    T = T_PAD
    xs = jnp.pad(x_sample, ((0, 0), (0, T - Ts), (0, 0))).reshape(DB * T, D)
    bias_col = jnp.repeat(sb_bias, T).reshape(SB_HEADS * T, 1)
    head_eye = jnp.eye(SB_HEADS, dtype=F32)

    def sb_sample_fn(pe):
        pe3 = pe.reshape(DB, T, -1)
        q = pe3[:, :, 3 * SB_W:4 * SB_W].reshape(DB, T, SB_HEADS, SB_DH)
        qbd = (q.transpose(0, 2, 1, 3)[:, :, :, None, :] * head_eye[None, :, None, :, None]).reshape(
            DB, SB_HEADS * T, SB_W)
        out = _sb_sample(qbd, bias_col, pe3[:, :, 4 * SB_W:5 * SB_W], pe3[:, :, 5 * SB_W:6 * SB_W],
                         cache_sb_k, cache_sb_v, page_table)
        return out.reshape(DB * T, SB_W)

    ck = cache_swa_k.reshape(DB, WINDOW, SW_KV)
    cv = cache_swa_v.reshape(DB, WINDOW, SW_KV)
    y_s, (c_s, n_s, m_s, pe_s, sre_s, sim_s, po_s) = _trunk(
        xs, (state_mlstm_C, state_mlstm_n, state_mlstm_m), (state_s5_re, state_s5_im),
        sb_sample_fn, functools.partial(_swa_sample, sinks=sinks, cache_k=ck, cache_v=cv, n_seq=DB, T=T),
        p, n_seq=DB, T=T, t_real=Ts)
    y_s = y_s.reshape(DB, T, D)[:, :Ts]
    pe_s3 = pe_s.reshape(DB, T, -1)[:, :Ts]
    sb_k_s = pe_s3[:, :, ksb0:ksb0 + SB_W].reshape(DB, Ts, SB_HEADS, SB_DH)
    sb_v_s = pe_s3[:, :, ksb0 + SB_W:].reshape(DB, Ts, SB_HEADS, SB_DH)
    po_s3 = po_s.reshape(DB, T, -1)[:, :Ts]
    k_new = po_s3[:, :, S5_CH + SW_Q:S5_CH + SW_Q + SW_KV].reshape(DB, Ts, SW_KV_HEADS, SW_DH)
    v_new = po_s3[:, :, S5_CH + SW_Q + SW_KV:].reshape(DB, Ts, SW_KV_HEADS, SW_DH)
    swa_k_s = jnp.concatenate([cache_swa_k[:, Ts:], k_new], axis=1)
    swa_v_s = jnp.concatenate([cache_swa_v[:, Ts:], v_new], axis=1)

    return (y_p.reshape(Bp, S, D), y_s, c_p, n_p, m_p, sb_k_p, sb_v_p, sre_p, sim_p, swa_k_p, swa_v_p,
            c_s, n_s, m_s, sb_k_s, sb_v_s, sre_s, sim_s, swa_k_s, swa_v_s)
```

```python
import functools
import math

import jax
import jax.numpy as jnp
import numpy as np
from jax import lax
from jax.experimental import pallas as pl
from jax.experimental.pallas import tpu as pltpu

F32 = jnp.float32
BF16 = jnp.bfloat16

D_MODEL = 1024
PAST_LEN = 8192
ML_HEADS, ML_DK, ML_DV = 4, 64, 128
ML_QK, ML_V = ML_HEADS * ML_DK, ML_HEADS * ML_DV
SB_HEADS, SB_DH = 8, 64
SB_W = SB_HEADS * SB_DH
S5_CH, S5_GROUP, S5_GROUPS, S5_STATE = 512, 16, 32, 64
S5_N = S5_GROUPS * S5_STATE
SW_HEADS, SW_KV_HEADS, SW_DH = 8, 2, 64
SW_Q, SW_KV = SW_HEADS * SW_DH, SW_KV_HEADS * SW_DH
SW_GROUP = SW_HEADS // SW_KV_HEADS
WINDOW = 128
D_FF = 4 * D_MODEL
EPS = 1e-6

LANES = 128
SUBLANES = 8
VMEM_LIMIT = 56 * 1024 * 1024
T_PAD = SUBLANES
ML_CHUNK = 128
SB_TILE = 128
PAGES_PER_STEP = 16
S5_TT = 32


def _params(sem):
    return pltpu.CompilerParams(dimension_semantics=sem, vmem_limit_bytes=VMEM_LIMIT)


def _dot(a, b):
    return jnp.dot(a, b, preferred_element_type=F32)


def _dot_nt(a, b):
    return lax.dot_general(a, b, (((1,), (1,)), ((), ())), preferred_element_type=F32)


def _dot_tn(a, b):
    return lax.dot_general(a, b, (((0,), (0,)), ((), ())), preferred_element_type=F32)


def _rms(x, g):
    ms = jnp.mean(x * x, axis=-1, keepdims=True)
    return x * lax.rsqrt(ms + EPS) * g


def _softplus(z):
    return jnp.maximum(z, 0.0) + jnp.log1p(jnp.exp(-jnp.abs(z)))


def _sigmoid(z):
    return 1.0 / (1.0 + jnp.exp(-z))


def _split_bf16(x):
    hi = x.astype(BF16)
    lo = (x - hi.astype(F32)).astype(BF16)
    return hi, lo


def _inproj_kernel(x_ref, g_ref, w_ref, o_ref, hn_ref):
    @pl.when(pl.program_id(1) == 0)
    def _():
        hn_ref[...] = _rms(x_ref[...], g_ref[...]).astype(BF16)

    o_ref[...] = _dot(hn_ref[...], w_ref[...])


def _inproj_gate_kernel(x_ref, g_ref, w_ref, wg_ref, gb_ref, o_ref, gate_ref, hn_ref):
    @pl.when(pl.program_id(1) == 0)
    def _():
        hn = _rms(x_ref[...], g_ref[...]).astype(BF16)
        hn_ref[...] = hn
        gate_ref[...] = _dot(hn, wg_ref[...]) + gb_ref[...]

    o_ref[...] = _dot(hn_ref[...], w_ref[...])


def _inproj(x, g, w, tn, wg=None, gb=None):
    M, D = x.shape
    N = w.shape[1]
    tm = min(512, M)
    grid = (M // tm, N // tn)
    x_spec = pl.BlockSpec((tm, D), lambda i, j: (i, 0))
    g_spec = pl.BlockSpec((1, D), lambda i, j: (0, 0))
    w_spec = pl.BlockSpec((D, tn), lambda i, j: (0, j))
    o_spec = pl.BlockSpec((tm, tn), lambda i, j: (i, j))
    scratch = [pltpu.VMEM((tm, D), BF16)]
    if wg is None:
        return pl.pallas_call(
            _inproj_kernel, grid=grid, in_specs=[x_spec, g_spec, w_spec], out_specs=o_spec,
            out_shape=jax.ShapeDtypeStruct((M, N), F32), scratch_shapes=scratch,
            compiler_params=_params(("parallel", "arbitrary")), name="inproj")(x, g, w)
    gate_spec = pl.BlockSpec((tm, LANES), lambda i, j: (i, 0))
    return pl.pallas_call(
        _inproj_gate_kernel, grid=grid,
        in_specs=[x_spec, g_spec, w_spec, pl.BlockSpec((D, LANES), lambda i, j: (0, 0)),
                  pl.BlockSpec((1, LANES), lambda i, j: (0, 0))],
        out_specs=[o_spec, gate_spec],
        out_shape=[jax.ShapeDtypeStruct((M, N), F32), jax.ShapeDtypeStruct((M, LANES), F32)],
        scratch_shapes=scratch, compiler_params=_params(("parallel", "arbitrary")),
        name="inproj_gate")(x, g, w, wg, gb)


def _inproj_tmajor_kernel(x_ref, g_ref, w_ref, o_ref, ut_ref):
    res = _dot(_rms(x_ref[...], g_ref[...]).astype(BF16), w_ref[...])
    o_ref[...] = res
    ut_ref[...] = res[:, :S5_CH]


def _inproj_tmajor(x, g, w, *, n_seq, T):
    M, D = x.shape
    N = w.shape[1]
    tm = 512
    nt = T // tm
    return pl.pallas_call(
        _inproj_tmajor_kernel, grid=(M // tm,),
        in_specs=[pl.BlockSpec((tm, D), lambda i: (i, 0)), pl.BlockSpec((1, D), lambda i: (0, 0)),
                  pl.BlockSpec((D, N), lambda i: (0, 0))],
        out_specs=[pl.BlockSpec((tm, N), lambda i: (i, 0)),
                   pl.BlockSpec((tm, S5_CH), lambda i: (i % nt, i // nt))],
        out_shape=[jax.ShapeDtypeStruct((M, N), F32), jax.ShapeDtypeStruct((T, n_seq * S5_CH), F32)],
        compiler_params=_params(("parallel",)), name="inproj_tmajor")(x, g, w)


def _mlstm_kernel(q_ref, k_ref, v_ref, o_ref, gt_ref, gh_ref, c0_ref, n0_ref, m0_ref,
                  h_ref, c_out, n_out, m_out, caug_scr, m_scr, *, G, L, last, nc):
    c = pl.program_id(1)
    first_chunk = (lambda f: f()) if nc == 1 else pl.when(c == 0)
    last_chunk = (lambda f: f()) if nc == 1 else pl.when(c == nc - 1)
    row = lax.broadcasted_iota(jnp.int32, (L, L), 0)
    col = lax.broadcasted_iota(jnp.int32, (L, L), 1)
    causal = col <= row
    eye = col == row
    e0 = (lax.broadcasted_iota(jnp.int32, (L, LANES), 1) == 0).astype(F32)
    r64 = lax.broadcasted_iota(jnp.int32, (ML_DK, ML_DK), 0)
    c64 = lax.broadcasted_iota(jnp.int32, (ML_DK, ML_DK), 1)
    eye64 = r64 == c64
    lane_dk = lax.broadcasted_iota(jnp.int32, (ML_DK, LANES), 1)
    lane_row = lax.broadcasted_iota(jnp.int32, (1, LANES), 1)
    rowid = lax.broadcasted_iota(jnp.int32, (L, 1), 0)

    def to_row(colvec):
        return jnp.sum(jnp.where(eye, colvec, 0.0), axis=0, keepdims=True)

    def seq_body(g, carry):
        r0 = g * L
        if not isinstance(r0, int):
            r0 = pl.multiple_of(r0, L)
        rows = pl.ds(r0, L)

        @first_chunk
        def _():
            for h in range(ML_HEADS):
                nrow = n0_ref[g, pl.ds(h, 1), :]
                ncol = jnp.sum(jnp.where(eye64, nrow, 0.0), axis=1, keepdims=True)
                caug_scr[g, h] = jnp.concatenate(
                    [c0_ref[g, h], jnp.where(lane_dk == 0, ncol, 0.0)], axis=1)
                m_scr[g, h] = jnp.broadcast_to(m0_ref[g][:, h:h + 1], (SUBLANES, LANES))

        gates = gt_ref[rows, :]
        m_row = jnp.zeros((1, LANES), F32)
        for h in range(ML_HEADS):
            q = q_ref[rows, pl.ds(h * ML_DK, ML_DK)]
            kc = k_ref[rows, pl.ds(h * ML_DK, ML_DK)] * (ML_DK ** -0.5)
            v = v_ref[rows, pl.ds(h * ML_DV, ML_DV)]
            icol = gates[:, h:h + 1]
            fcol = gates[:, ML_HEADS + h:ML_HEADS + h + 1]
            caug = caug_scr[g, h]
            m_prev = m_scr[g, h][0:1, 0:1]

            lfcol = jnp.minimum(fcol, 0.0) - jnp.log1p(jnp.exp(-jnp.abs(fcol)))
            lfrow = to_row(lfcol)
            irow = to_row(icol)
            bcol = jnp.sum(jnp.where(causal, lfrow, 0.0), axis=1, keepdims=True)
            brow = to_row(bcol)
            dmat = jnp.where(causal, bcol - brow + irow, -jnp.inf)
            inter = bcol + m_prev
            mt = jnp.maximum(inter, jnp.max(dmat, axis=1, keepdims=True))
            w_inter = jnp.exp(inter - mt)
            qb = q.astype(BF16)
            s = _dot_nt(qb, kc.astype(BF16)) * jnp.exp(dmat - mt)
            vaug = jnp.concatenate([v, e0], axis=1).astype(BF16)
            tot = w_inter * _dot(qb, caug.astype(BF16)) + _dot(s.astype(BF16), vaug)
            num = tot[:, :ML_DV]
            den = tot[:, ML_DV:ML_DV + 1]
            hh = num / jnp.maximum(jnp.abs(den), jnp.exp(-mt))

            m_new = mt[last:last + 1, :]
            g_state = jnp.exp(inter[last:last + 1, :] - m_new)
            gin = jnp.exp(bcol[last:last + 1, :] - bcol + icol - m_new)
            if last < L - 1:
                gin = jnp.where(rowid <= last, gin, 0.0)
            caug_new = g_state * caug + _dot_tn((kc * gin).astype(BF16), vaug)
            caug_scr[g, h] = caug_new
            m_scr[g, h] = jnp.broadcast_to(m_new, (SUBLANES, LANES))
            m_row = m_row + jnp.where(lane_row == h, m_new, 0.0)

            hh = hh * lax.rsqrt(jnp.mean(hh * hh, axis=-1, keepdims=True) + EPS)
            hh = hh * gh_ref[:, pl.ds(h * ML_DV, ML_DV)]
            hh = hh * _sigmoid(o_ref[rows, pl.ds(h * ML_DV, ML_DV)])
            h_ref[rows, pl.ds(h * ML_DV, ML_DV)] = hh

        @last_chunk
        def _():
            for h in range(ML_HEADS):
                caug_fin = caug_scr[g, h]
                c_out[g, h] = caug_fin[:, :ML_DV]
                ncol = caug_fin[:, ML_DV:ML_DV + 1]
                n_out[g, pl.ds(h, 1), :] = jnp.sum(jnp.where(eye64, ncol, 0.0), axis=0, keepdims=True)
            m_out[g] = m_row

        return carry

    if G == 1:
        seq_body(0, 0)
    else:
        lax.fori_loop(0, G, seq_body, 0, unroll=2)


def _mlstm(pe, gates, g_head, c0, n0, m0, *, n_seq, seq_len, last):
    M = pe.shape[0]
    if seq_len >= ML_CHUNK:
        G, L = 1, ML_CHUNK
    else:
        L = seq_len
        G = ML_CHUNK // L
    nc = seq_len // L
    last_l = last - (nc - 1) * L
    grid = (n_seq // G, nc)
    rb = lambda b, c: b * nc + c
    R = G * L
    kernel = functools.partial(_mlstm_kernel, G=G, L=L, last=last_l, nc=nc)
    return pl.pallas_call(
        kernel, grid=grid,
        in_specs=[
            pl.BlockSpec((R, ML_QK), lambda b, c: (rb(b, c), 0)),
            pl.BlockSpec((R, ML_QK), lambda b, c: (rb(b, c), 1)),
            pl.BlockSpec((R, ML_V), lambda b, c: (rb(b, c), 1)),
            pl.BlockSpec((R, ML_V), lambda b, c: (rb(b, c), 2)),
            pl.BlockSpec((R, LANES), lambda b, c: (rb(b, c), 0)),
            pl.BlockSpec((1, ML_V), lambda b, c: (0, 0)),
            pl.BlockSpec((G, ML_HEADS, ML_DK, ML_DV), lambda b, c: (b, 0, 0, 0)),
            pl.BlockSpec((G, ML_HEADS, ML_DK), lambda b, c: (b, 0, 0)),
            pl.BlockSpec((G, 1, ML_HEADS), lambda b, c: (b, 0, 0)),
        ],
        out_specs=[
            pl.BlockSpec((R, ML_V), lambda b, c: (rb(b, c), 0)),
            pl.BlockSpec((G, ML_HEADS, ML_DK, ML_DV), lambda b, c: (b, 0, 0, 0)),
            pl.BlockSpec((G, ML_HEADS, ML_DK), lambda b, c: (b, 0, 0)),
            pl.BlockSpec((G, 1, LANES), lambda b, c: (b, 0, 0)),
        ],
        out_shape=[
            jax.ShapeDtypeStruct((M, ML_V), F32),
            jax.ShapeDtypeStruct((n_seq, ML_HEADS, ML_DK, ML_DV), F32),
            jax.ShapeDtypeStruct((n_seq, ML_HEADS, ML_DK), F32),
            jax.ShapeDtypeStruct((n_seq, 1, LANES), F32),
        ],
        scratch_shapes=[pltpu.VMEM((G, ML_HEADS, ML_DK, 2 * ML_DV), F32),
                        pltpu.VMEM((G, ML_HEADS, SUBLANES, LANES), F32)],
        compiler_params=_params(("parallel", "arbitrary")), name="mlstm",
    )(pe, pe, pe, pe, gates, g_head, c0, n0, m0.reshape(n_seq, 1, ML_HEADS))


def _suffix_matrix(n):
    r = lax.broadcasted_iota(jnp.int32, (n, n), 0)
    c = lax.broadcasted_iota(jnp.int32, (n, n), 1)
    return (r > c).astype(BF16)


def _sb_tile(z, valid, lrem, u_mat):
    sp = _softplus(z)
    lk = -sp if valid is None else jnp.where(valid, -sp, 0.0)
    hi, lo = _split_bf16(lk)
    later = _dot(hi, u_mat) + _dot(lo, u_mat)
    w = jnp.exp(z - sp + later + lrem)
    if valid is not None:
        w = jnp.where(valid, w, 0.0)
    return w, lrem + jnp.sum(lk, axis=1, keepdims=True)


def _suffix_rows(x):
    K, Q = x.shape
    sub = lax.broadcasted_iota(jnp.int32, (SUBLANES, Q), 0)
    outs = [None] * (K // SUBLANES)
    carry = jnp.zeros((SUBLANES, Q), F32)
    for j in reversed(range(K // SUBLANES)):
        xb = x[j * SUBLANES:(j + 1) * SUBLANES, :]
        y = xb
        for step in (1, 2, 4):
            y = y + jnp.where(sub < SUBLANES - step, pltpu.roll(y, SUBLANES - step, 0), 0.0)
        outs[j] = y - xb + carry
        carry = carry + jnp.broadcast_to(y[0:1, :], (SUBLANES, Q))
    return jnp.concatenate(outs, axis=0), carry


def _sb_prompt_kernel(bias_ref, q_ref, k_ref, v_ref, o_ref, ks_ref, vs_ref, qt_ref, acc_ref, *, T):
    qi = pl.program_id(1)

    @pl.when(qi == 0)
    def _():
        for h in range(SB_HEADS):
            ks_ref[h] = k_ref[:, pl.ds(h * SB_DH, SB_DH)].astype(BF16)
            vs_ref[h] = v_ref[:, pl.ds(h * SB_DH, SB_DH)].astype(BF16)

    for hp in range(SB_HEADS // 2):
        qt = (q_ref[:, pl.ds(hp * LANES, LANES)] * (SB_DH ** -0.5)).T
        qt_ref[2 * hp] = qt[:SB_DH].astype(BF16)
        qt_ref[2 * hp + 1] = qt[SB_DH:].astype(BF16)

    row = lax.broadcasted_iota(jnp.int32, (T, T), 0)
    col = lax.broadcasted_iota(jnp.int32, (T, T), 1)
    strict = row < col

    def tile(h, j, lrem, valid):
        ks = pl.ds(pl.multiple_of(j * T, T), T)
        zt = _dot(ks_ref[h, ks, :], qt_ref[h]) + bias_ref[h]
        sp = jnp.maximum(zt, 0.0) + jnp.log(1.0 + jnp.exp(-jnp.abs(zt)))
        spm = sp if valid is None else jnp.where(valid, sp, 0.0)
        later, tot = _suffix_rows(spm)
        w = jnp.exp(zt - sp - later - lrem[0:1, :])
        if valid is not None:
            w = jnp.where(valid, w, 0.0)
        return _dot_tn(w.astype(BF16), vs_ref[h, ks, :]), lrem + tot

    lrems = []
    for h in range(SB_HEADS):
        pv, lrem = tile(h, qi, jnp.zeros((SUBLANES, T), F32), strict)
        acc_ref[h] = pv
        lrems.append(lrem)

    def body(it, lrems):
        out = []
        for h in range(SB_HEADS):
            pv, lrem = tile(h, qi - 1 - it, lrems[h], None)
            acc_ref[h] += pv
            out.append(lrem)
        return tuple(out)

    lax.fori_loop(0, qi, body, tuple(lrems))
    o_ref[...] = jnp.concatenate([acc_ref[h] for h in range(SB_HEADS)], axis=1)


def _sb_prompt(pe, sb_bias, *, n_seq, seq_len):
    M = pe.shape[0]
    T = SB_TILE
    nq = seq_len // T
    kernel = functools.partial(_sb_prompt_kernel, T=T)
    return pl.pallas_call(
        kernel, grid=(n_seq, nq),
        in_specs=[
            pl.BlockSpec(memory_space=pltpu.SMEM),
            pl.BlockSpec((T, SB_W), lambda b, i: (b * nq + i, 3)),
            pl.BlockSpec((seq_len, SB_W), lambda b, i: (b, 4)),
            pl.BlockSpec((seq_len, SB_W), lambda b, i: (b, 5)),
        ],
        out_specs=pl.BlockSpec((T, SB_W), lambda b, i: (b * nq + i, 0)),
        out_shape=jax.ShapeDtypeStruct((M, SB_W), F32),
        scratch_shapes=[pltpu.VMEM((SB_HEADS, seq_len, SB_DH), BF16),
                        pltpu.VMEM((SB_HEADS, seq_len, SB_DH), BF16),
                        pltpu.VMEM((SB_HEADS, SB_DH, T), BF16),
                        pltpu.VMEM((SB_HEADS, T, SB_DH), F32)],
        compiler_params=_params(("parallel", "arbitrary")), name="sb_prompt",
    )(sb_bias, pe, pe, pe)


def _sb_sample_kernel(pt_ref, bias_ref, qbd_ref, kn_ref, vn_ref, *refs, P, T):
    k_refs, v_refs = refs[:P], refs[P:2 * P]
    o_ref, acc_ref, lrem_ref = refs[2 * P], refs[2 * P + 1], refs[2 * P + 2]
    s = pl.program_id(1)
    R = SB_HEADS * T
    u_mat = _suffix_matrix(LANES)
    qb = (qbd_ref[0] * (SB_DH ** -0.5)).astype(BF16)
    bias = bias_ref[...]

    @pl.when(s == 0)
    def _():
        pad = jnp.zeros((LANES - T, SB_W), F32)
        kn = jnp.concatenate([kn_ref[0], pad], axis=0).astype(BF16)
        vn = jnp.concatenate([vn_ref[0], pad], axis=0).astype(BF16)
        t_of_row = lax.broadcasted_iota(jnp.int32, (R, LANES), 0) % T
        colk = lax.broadcasted_iota(jnp.int32, (R, LANES), 1)
        z = _dot_nt(qb, kn) + bias
        w, lrem = _sb_tile(z, colk < t_of_row, jnp.zeros((R, 1), F32), u_mat)
        acc_ref[...] = _dot(w.astype(BF16), vn)
        lrem_ref[...] = jnp.broadcast_to(lrem, (R, LANES))

    acc = acc_ref[...]
    lrem = lrem_ref[:, 0:1]
    kt = jnp.concatenate([k_refs[i][0].reshape(SB_W, LANES) for i in range(P)], axis=1).astype(BF16)
    z_all = _dot(qb, kt)
    z = jnp.concatenate([z_all[:, i * LANES:(i + 1) * LANES] for i in range(P)], axis=0)
    z = z + jnp.concatenate([bias] * P, axis=0)
    sp = _softplus(z)
    hi, lo = _split_bf16(-sp)
    later = _dot(hi, u_mat) + _dot(lo, u_mat)
    tot = jnp.sum(sp, axis=1, keepdims=True)
    lrems = []
    for i in range(P):
        lrems.append(lrem)
        lrem = lrem - tot[i * R:(i + 1) * R]
    w = jnp.exp(z - sp + later + jnp.concatenate(lrems, axis=0)).astype(BF16)
    w_all = jnp.concatenate([w[i * R:(i + 1) * R] for i in range(P)], axis=1)
    v = jnp.concatenate([v_refs[i][0].reshape(SB_W, LANES).T for i in range(P)], axis=0).astype(BF16)
    acc = acc + _dot(w_all, v)
    acc_ref[...] = acc
    lrem_ref[...] = jnp.broadcast_to(lrem, (R, LANES))

    @pl.when(s == pl.num_programs(1) - 1)
    def _():
        o_ref[0] = jnp.concatenate(
            [acc[h * T:(h + 1) * T, h * SB_DH:(h + 1) * SB_DH] for h in range(SB_HEADS)], axis=1)


def _sb_sample(qbd, bias_col, k_new, v_new, cache_k, cache_v, page_table):
    DB, R, _ = qbd.shape
    T = R // SB_HEADS
    n_pages = page_table.shape[1]
    P = PAGES_PER_STEP
    page = cache_k.shape[1]
    assert page == LANES
    ck = cache_k.transpose(0, 2, 3, 1)
    cv = cache_v.transpose(0, 2, 3, 1)

    def page_spec(i):
        return pl.BlockSpec((1, SB_HEADS, SB_DH, page),
                            lambda b, s, pt: (pt[b, n_pages - 1 - (s * P + i)], 0, 0, 0))

    grid_spec = pltpu.PrefetchScalarGridSpec(
        num_scalar_prefetch=1, grid=(DB, n_pages // P),
        in_specs=[pl.BlockSpec((R, 1), lambda b, s, pt: (0, 0)),
                  pl.BlockSpec((1, R, SB_W), lambda b, s, pt: (b, 0, 0)),
                  pl.BlockSpec((1, T, SB_W), lambda b, s, pt: (b, 0, 0)),
                  pl.BlockSpec((1, T, SB_W), lambda b, s, pt: (b, 0, 0))]
        + [page_spec(i) for i in range(P)] + [page_spec(i) for i in range(P)],
        out_specs=pl.BlockSpec((1, T, SB_W), lambda b, s, pt: (b, 0, 0)),
        scratch_shapes=[pltpu.VMEM((R, SB_W), F32), pltpu.VMEM((R, LANES), F32)])
    kernel = functools.partial(_sb_sample_kernel, P=P, T=T)
    return pl.pallas_call(
        kernel, grid_spec=grid_spec, out_shape=jax.ShapeDtypeStruct((DB, T, SB_W), F32),
        compiler_params=_params(("parallel", "arbitrary")), name="sb_sample",
    )(page_table, bias_col, qbd, k_new, v_new, *([ck] * P), *([cv] * P))


def _s5_param_kernel(lre_ref, lim_ref, ldt_ref, bre_ref, bim_ref, are_ref, aim_ref, bbre_ref, bbim_ref):
    lre = jnp.minimum(lre_ref[...], -1e-4)
    lim = lim_ref[...]
    dt = jnp.exp(ldt_ref[...])
    mag = jnp.exp(lre * dt)
    ab_re = mag * jnp.cos(lim * dt)
    ab_im = mag * jnp.sin(lim * dt)
    den = lre * lre + lim * lim
    c_re = ((ab_re - 1.0) * lre + ab_im * lim) / den
    c_im = (ab_im * lre - (ab_re - 1.0) * lim) / den
    b_re, b_im = bre_ref[...], bim_ref[...]
    are_ref[...] = ab_re
    aim_ref[...] = ab_im
    bbre_ref[...] = c_re * b_re - c_im * b_im
    bbim_ref[...] = c_re * b_im + c_im * b_re


def _s5_params(lam_re, lam_im, log_dt, b_re, b_im):
    N = S5_N
    col = lambda a: a.reshape(N, 1)
    ldt = jnp.repeat(log_dt, S5_STATE).reshape(N, 1)
    full = lambda shape: pl.BlockSpec(shape, lambda: (0,) * len(shape))
    are, aim, bbre, bbim = pl.pallas_call(
        _s5_param_kernel,
        in_specs=[full((N, 1))] * 3 + [full((N, S5_GROUP))] * 2,
        out_specs=[full((N, 1))] * 2 + [full((N, S5_GROUP))] * 2,
        out_shape=[jax.ShapeDtypeStruct((N, 1), F32)] * 2 + [jax.ShapeDtypeStruct((N, S5_GROUP), F32)] * 2,
        name="s5_params",
    )(col(lam_re), col(lam_im), ldt, b_re.reshape(N, S5_GROUP), b_im.reshape(N, S5_GROUP))
    return are.reshape(1, N), aim.reshape(1, N), bbre, bbim


def _block_diag(blocks):
    G, r, c = blocks.shape
    eye = jnp.eye(G, dtype=blocks.dtype)
    return (blocks[:, :, None, :] * eye[:, None, :, None]).reshape(G * r, G * c)


def _s5_kernel(u_ref, are_ref, aim_ref, bb_ref, cre_ref, cim_ref, d_ref, wg_ref, bg_ref, sre0_ref, sim0_ref,
               y_ref, sre_out, sim_out, bu_scr, xs_scr, st_scr, *, TT, t_last):
    tb = pl.program_id(1)
    N = S5_N
    CH = 512
    BS = SUBLANES

    @pl.when(tb == 0)
    def _():
        st_scr[:, 0:N] = sre0_ref[...]
        st_scr[:, N:2 * N] = sim0_ref[...]

    u = u_ref[...].reshape(TT * BS, S5_CH)
    bu_scr[...] = _dot(u.astype(BF16), bb_ref[...])

    for cidx in range(N // CH):
        lo = cidx * CH
        ar = jnp.broadcast_to(are_ref[:, lo:lo + CH], (BS, CH))
        ai = jnp.broadcast_to(aim_ref[:, lo:lo + CH], (BS, CH))

        def step(t, carry):
            xr, xi = carry
            r = pl.ds(pl.multiple_of(t * BS, BS), BS)
            br = bu_scr[r, lo:lo + CH]
            bi = bu_scr[r, N + lo:N + lo + CH]
            nr = ar * xr - ai * xi + br
            ni = ar * xi + ai * xr + bi
            xs_scr[r, lo:lo + CH] = nr
            xs_scr[r, N + lo:N + lo + CH] = ni
            return nr, ni

        xr, xi = lax.fori_loop(0, TT, step, (st_scr[:, lo:lo + CH], st_scr[:, N + lo:N + lo + CH]))
        st_scr[:, lo:lo + CH] = xr
        st_scr[:, N + lo:N + lo + CH] = xi

    @pl.when(tb == t_last // TT)
    def _():
        r = pl.ds((t_last % TT) * BS, BS)
        sre_out[...] = xs_scr[r, 0:N]
        sim_out[...] = xs_scr[r, N:2 * N]

    xs = xs_scr[...]
    y = (_dot(xs[:, 0:N].astype(BF16), cre_ref[...]) - _dot(xs[:, N:2 * N].astype(BF16), cim_ref[...])
         + d_ref[...] * u)
    y = jax.nn.gelu(y)
    y = y * _sigmoid(_dot(y.astype(BF16), wg_ref[...]) + bg_ref[...])
    y_ref[...] = y.reshape(TT, BS, S5_CH)


def _s5(u_t, are, aim, bb, cre, cim, d, wg, bg, sre0, sim0, *, t_last):
    T, B, _ = u_t.shape
    TT = min(S5_TT, T)
    N = S5_N
    const = lambda shape: pl.BlockSpec(shape, lambda b, t: (0,) * len(shape))
    kernel = functools.partial(_s5_kernel, TT=TT, t_last=t_last)
    return pl.pallas_call(
        kernel, grid=(B // SUBLANES, T // TT),
        in_specs=[pl.BlockSpec((TT, SUBLANES, S5_CH), lambda b, t: (t, b, 0)),
                  const((1, N)), const((1, N)), const((S5_CH, 2 * N)), const((N, S5_CH)), const((N, S5_CH)),
                  const((1, S5_CH)), const((S5_CH, S5_CH)), const((1, S5_CH)),
                  pl.BlockSpec((SUBLANES, N), lambda b, t: (b, 0)),
                  pl.BlockSpec((SUBLANES, N), lambda b, t: (b, 0))],
        out_specs=[pl.BlockSpec((TT, SUBLANES, S5_CH), lambda b, t: (t, b, 0)),
                   pl.BlockSpec((SUBLANES, N), lambda b, t: (b, 0)),
                   pl.BlockSpec((SUBLANES, N), lambda b, t: (b, 0))],
        out_shape=[jax.ShapeDtypeStruct((T, B, S5_CH), F32),
                   jax.ShapeDtypeStruct((B, N), F32), jax.ShapeDtypeStruct((B, N), F32)],
        scratch_shapes=[pltpu.VMEM((TT * SUBLANES, 2 * N), F32), pltpu.VMEM((TT * SUBLANES, 2 * N), F32),
                        pltpu.VMEM((SUBLANES, 2 * N), F32)],
        compiler_params=_params(("parallel", "arbitrary")), name="s5",
    )(u_t, are, aim, bb, cre, cim, d, wg, bg, sre0, sim0)


def _alibi_slope(h):
    return float(2.0 ** (-8.0 * (h + 1) / SW_HEADS))


def _sink_softmax_parts(zs, sink):
    m = sink
    for z in zs:
        m = jnp.maximum(m, jnp.max(z, axis=-1, keepdims=True))
    es = [jnp.exp(z - m) for z in zs]
    tot = jnp.exp(sink - m)
    for e in es:
        tot = tot + jnp.sum(e, axis=-1, keepdims=True)
    return [e / tot for e in es]


def _swa_prompt_kernel(sink_ref, q_ref, kp_ref, kc_ref, vp_ref, vc_ref, o_ref):
    i = pl.program_id(1)
    W = WINDOW
    rowq = lax.broadcasted_iota(jnp.int32, (W, 2 * W), 0)
    colc = lax.broadcasted_iota(jnp.int32, (W, 2 * W), 1)
    dist = W + rowq - colc
    valid = (dist >= 0) & (dist < W) & (colc >= jnp.where(i > 0, 0, W))
    distf = dist.astype(F32)
    outs = []
    for kv in range(SW_KV_HEADS):
        ls = pl.ds(kv * SW_DH, SW_DH)
        kband = jnp.concatenate([kp_ref[:, ls], kc_ref[:, ls]], axis=0).astype(BF16)
        vband = jnp.concatenate([vp_ref[:, ls], vc_ref[:, ls]], axis=0).astype(BF16)
        for gq in range(SW_GROUP):
            h = kv * SW_GROUP + gq
            qh = q_ref[:, pl.ds(h * SW_DH, SW_DH)].astype(BF16)
            z = _dot_nt(qh, kband) * (SW_DH ** -0.5) - _alibi_slope(h) * distf
            z = jnp.where(valid, z, -jnp.inf)
            (p,) = _sink_softmax_parts([z], sink_ref[h])
            outs.append(_dot(p.astype(BF16), vband))
    o_ref[...] = jnp.concatenate(outs, axis=1)


def _swa_prompt(po, sinks, *, n_seq, seq_len):
    M = po.shape[0]
    W = WINDOW
    nb = seq_len // W
    kcol, vcol = (S5_CH + SW_Q) // SW_KV, (S5_CH + SW_Q) // SW_KV + 1
    cur = lambda b, i: b * nb + i
    prev = lambda b, i: b * nb + jnp.maximum(i - 1, 0)
    return pl.pallas_call(
        _swa_prompt_kernel, grid=(n_seq, nb),
        in_specs=[pl.BlockSpec(memory_space=pltpu.SMEM),
                  pl.BlockSpec((W, SW_Q), lambda b, i: (cur(b, i), 1)),
                  pl.BlockSpec((W, SW_KV), lambda b, i: (prev(b, i), kcol)),
                  pl.BlockSpec((W, SW_KV), lambda b, i: (cur(b, i), kcol)),
                  pl.BlockSpec((W, SW_KV), lambda b, i: (prev(b, i), vcol)),
                  pl.BlockSpec((W, SW_KV), lambda b, i: (cur(b, i), vcol))],
        out_specs=pl.BlockSpec((W, SW_Q), lambda b, i: (cur(b, i), 0)),
        out_shape=jax.ShapeDtypeStruct((M, SW_Q), F32),
        compiler_params=_params(("parallel", "arbitrary")), name="swa_prompt",
    )(sinks, po, po, po, po, po)


def _swa_sample_kernel(sink_ref, q_ref, kn_ref, vn_ref, ck_ref, cv_ref, o_ref, *, G, T):
    W = WINDOW
    R = SW_GROUP * T
    t_c = lax.broadcasted_iota(jnp.int32, (R, W), 0) % T
    j_c = lax.broadcasted_iota(jnp.int32, (R, W), 1)
    dist_c = W + t_c - j_c
    valid_c = dist_c < W
    t_n = lax.broadcasted_iota(jnp.int32, (R, T), 0) % T
    s_n = lax.broadcasted_iota(jnp.int32, (R, T), 1)
    dist_n = t_n - s_n
    valid_n = dist_n >= 0
    g_of_row = lax.broadcasted_iota(jnp.int32, (R, 1), 0) // T

    def seq_body(g, carry):
        rows = pl.ds(pl.multiple_of(g * T, T), T)
        outs = []
        for kv in range(SW_KV_HEADS):
            ls = pl.ds(kv * SW_DH, SW_DH)
            slope = jnp.zeros((R, 1), F32)
            sink = jnp.zeros((R, 1), F32)
            for gq in range(SW_GROUP):
                h = kv * SW_GROUP + gq
                slope = jnp.where(g_of_row == gq, _alibi_slope(h), slope)
                sink = jnp.where(g_of_row == gq, sink_ref[h], sink)
            q4 = jnp.concatenate(
                [q_ref[rows, pl.ds((kv * SW_GROUP + gq) * SW_DH, SW_DH)] for gq in range(SW_GROUP)],
                axis=0).astype(BF16)
            kc = ck_ref[g, :, ls].astype(BF16)
            vc = cv_ref[g, :, ls].astype(BF16)
            kn = kn_ref[rows, ls].astype(BF16)
            vn = vn_ref[rows, ls].astype(BF16)
            zc = _dot_nt(q4, kc) * (SW_DH ** -0.5) - slope * dist_c.astype(F32)
            zn = _dot_nt(q4, kn) * (SW_DH ** -0.5) - slope * dist_n.astype(F32)
            zc = jnp.where(valid_c, zc, -jnp.inf)
            zn = jnp.where(valid_n, zn, -jnp.inf)
            pc, pn = _sink_softmax_parts([zc, zn], sink)
            o4 = _dot(pc.astype(BF16), vc) + _dot(pn.astype(BF16), vn)
            outs += [o4[gq * T:(gq + 1) * T] for gq in range(SW_GROUP)]
        o_ref[rows, :] = jnp.concatenate(outs, axis=1)
        return carry

    lax.fori_loop(0, G, seq_body, 0)


def _swa_sample(po, sinks, cache_k, cache_v, *, n_seq, T):
    M = po.shape[0]
    G = 16
    kcol, vcol = (S5_CH + SW_Q) // SW_KV, (S5_CH + SW_Q) // SW_KV + 1
    kernel = functools.partial(_swa_sample_kernel, G=G, T=T)
    return pl.pallas_call(
        kernel, grid=(n_seq // G,),
        in_specs=[pl.BlockSpec(memory_space=pltpu.SMEM),
                  pl.BlockSpec((G * T, SW_Q), lambda b: (b, 1)),
                  pl.BlockSpec((G * T, SW_KV), lambda b: (b, kcol)),
                  pl.BlockSpec((G * T, SW_KV), lambda b: (b, vcol)),
                  pl.BlockSpec((G, WINDOW, SW_KV), lambda b: (b, 0, 0)),
                  pl.BlockSpec((G, WINDOW, SW_KV), lambda b: (b, 0, 0))],
        out_specs=pl.BlockSpec((G * T, SW_Q), lambda b: (b, 0)),
        out_shape=jax.ShapeDtypeStruct((M, SW_Q), F32),
        compiler_params=_params(("parallel",)), name="swa_sample",
    )(sinks, po, po, po, cache_k, cache_v)


def _post_kernel(x_ref, a_ref, b_ref, wa_ref, wb_ref, g_ref, w1_ref, w2_ref, gf_ref, o_ref,
                 x1_scr, hn_scr, acc_scr, *, final):
    j = pl.program_id(1)

    @pl.when(j == 0)
    def _():
        x1 = x_ref[...] + _dot(a_ref[...].astype(BF16), wa_ref[...]) + _dot(b_ref[...].astype(BF16), wb_ref[...])
        x1_scr[...] = x1
        hn_scr[...] = _rms(x1, g_ref[...]).astype(BF16)
        acc_scr[...] = jnp.zeros_like(acc_scr)

    hmid = jnp.square(jnp.maximum(_dot(hn_scr[...], w1_ref[...]), 0.0))
    acc_scr[...] += _dot(hmid.astype(BF16), w2_ref[...])

    @pl.when(j == pl.num_programs(1) - 1)
    def _():
        y = x1_scr[...] + acc_scr[...]
        if final:
            y = _rms(y, gf_ref[...])
        o_ref[...] = y


def _post(x, a, b, wa, wb, g, w1, w2, gf, *, final, a_tmajor_seq_len=None):
    M, D = x.shape
    tm = min(512, M)
    tf = 1024
    Ka, Kb = wa.shape[0], wb.shape[0]
    if a_tmajor_seq_len is None:
        a_map = lambda i, j: (i, 0)
    else:
        nt = a_tmajor_seq_len // tm
        a_map = lambda i, j: (i % nt, i // nt)
    kernel = functools.partial(_post_kernel, final=final)
    return pl.pallas_call(
        kernel, grid=(M // tm, D_FF // tf),
        in_specs=[pl.BlockSpec((tm, D), lambda i, j: (i, 0)),
                  pl.BlockSpec((tm, Ka), a_map),
                  pl.BlockSpec((tm, Kb), lambda i, j: (i, 0)),
                  pl.BlockSpec((Ka, D), lambda i, j: (0, 0)),
                  pl.BlockSpec((Kb, D), lambda i, j: (0, 0)),
                  pl.BlockSpec((1, D), lambda i, j: (0, 0)),
                  pl.BlockSpec((D, tf), lambda i, j: (0, j)),
                  pl.BlockSpec((tf, D), lambda i, j: (j, 0)),
                  pl.BlockSpec((1, D), lambda i, j: (0, 0))],
        out_specs=pl.BlockSpec((tm, D), lambda i, j: (i, 0)),
        out_shape=jax.ShapeDtypeStruct((M, D), F32),
        scratch_shapes=[pltpu.VMEM((tm, D), F32), pltpu.VMEM((tm, D), BF16), pltpu.VMEM((tm, D), F32)],
        compiler_params=_params(("parallel", "arbitrary")), name="post",
    )(x, a, b, wa, wb, g, w1, w2, gf)


def _trunk(x, ml_state, s5_state, sb_fn, swa_fn, p, *, n_seq, T, t_real):
    last = t_real - 1
    pe, gates = _inproj(x, p["g_mix0"], p["w_even"], 1024, p["w_gate"], p["b_gate"])
    h_ml, c_new, n_new, m_new = _mlstm(pe, gates, p["g_head"], *ml_state, n_seq=n_seq, seq_len=T, last=last)
    h_sb = sb_fn(pe)
    x = _post(x, h_ml, h_sb, p["wo_even_a"], p["wo_even_b"], p["g_ffn0"], p["w1_0"], p["w2_0"],
              p["g_final"], final=False)
    tmajor = T % 512 == 0
    if tmajor:
        po, u_t = _inproj_tmajor(x, p["g_mix1"], p["w_odd"], n_seq=n_seq, T=T)
        u_t = u_t.reshape(T, n_seq, S5_CH)
    else:
        po = _inproj(x, p["g_mix1"], p["w_odd"], 640)
        u_t = po[:, :S5_CH].reshape(n_seq, T, S5_CH).transpose(1, 0, 2)
    y_t, s_re, s_im = _s5(u_t, p["a_re"], p["a_im"], p["bb"], p["cc_re"], p["cc_im"], p["s5_d"], p["w_glu"],
                          p["b_glu"], s5_state[0].reshape(n_seq, S5_N), s5_state[1].reshape(n_seq, S5_N),
                          t_last=last)
    o_sw = swa_fn(po)
    if tmajor:
        y_s5, seq_len = y_t.reshape(T, n_seq * S5_CH), T
    else:
        y_s5, seq_len = y_t.transpose(1, 0, 2).reshape(n_seq * T, S5_CH), None
    x = _post(x, y_s5, o_sw, p["wo_odd_a"], p["wo_odd_b"], p["g_ffn1"], p["w1_1"], p["w2_1"],
              p["g_final"], final=True, a_tmajor_seq_len=seq_len)
    state = (c_new, n_new, m_new[:, 0, :ML_HEADS], pe, s_re.reshape(n_seq, S5_GROUPS, S5_STATE),
             s_im.reshape(n_seq, S5_GROUPS, S5_STATE), po)
    return x, state


def kernel(x_prompt, x_sample, state_mlstm_C, state_mlstm_n, state_mlstm_m, cache_sb_k, cache_sb_v, page_table, state_s5_re, state_s5_im, cache_swa_k, cache_swa_v, g_norm_mix, g_norm_ffn, g_norm_final, w_in_even, b_igate, b_fgate, g_mlstm_head, sb_bias, w_out_even, w_in_odd, s5_lambda_re, s5_lambda_im, s5_log_dt, s5_B_re, s5_B_im, s5_C_re, s5_C_im, s5_D, w_glu, b_glu, swa_sinks, w_out_odd, w_ff1, w_ff2):
    Bp, S, D = x_prompt.shape
    DB, Ts, _ = x_sample.shape
    n_gate = 2 * ML_HEADS
    g0 = 2 * ML_QK + 2 * ML_V
    row = lambda a: a.reshape(1, -1).astype(F32)

    a_re, a_im, bb_re, bb_im = _s5_params(s5_lambda_re, s5_lambda_im, s5_log_dt, s5_B_re, s5_B_im)
    bbt = lambda bb: _block_diag(bb.reshape(S5_GROUPS, S5_STATE, S5_GROUP).transpose(0, 2, 1))
    p = {
        "g_mix0": row(g_norm_mix[0]), "g_mix1": row(g_norm_mix[1]),
        "g_ffn0": row(g_norm_ffn[0]), "g_ffn1": row(g_norm_ffn[1]), "g_final": row(g_norm_final),
        "w_even": jnp.concatenate([w_in_even[:, :g0], w_in_even[:, g0 + n_gate:]], axis=1).astype(BF16),
        "w_gate": jnp.pad(w_in_even[:, g0:g0 + n_gate], ((0, 0), (0, LANES - n_gate))).astype(BF16),
        "b_gate": jnp.pad(jnp.concatenate([b_igate, b_fgate]), (0, LANES - n_gate)).reshape(1, LANES).astype(F32),
        "g_head": row(g_mlstm_head),
        "wo_even_a": w_out_even[:ML_V].astype(BF16), "wo_even_b": w_out_even[ML_V:].astype(BF16),
        "w_odd": w_in_odd.astype(BF16),
        "a_re": a_re, "a_im": a_im,
        "bb": jnp.concatenate([bbt(bb_re), bbt(bb_im)], axis=1).astype(BF16),
        "cc_re": _block_diag(s5_C_re.transpose(0, 2, 1)).astype(BF16),
        "cc_im": _block_diag(s5_C_im.transpose(0, 2, 1)).astype(BF16),
        "s5_d": row(s5_D), "w_glu": w_glu.astype(BF16), "b_glu": row(b_glu),
        "wo_odd_a": w_out_odd[:S5_CH].astype(BF16), "wo_odd_b": w_out_odd[S5_CH:].astype(BF16),
        "w1_0": w_ff1[0].astype(BF16), "w2_0": w_ff2[0].astype(BF16),
        "w1_1": w_ff1[1].astype(BF16), "w2_1": w_ff2[1].astype(BF16),
    }
    sb_bias = sb_bias.astype(F32)
    sinks = swa_sinks.astype(F32)

    ml0 = (jnp.zeros((Bp, ML_HEADS, ML_DK, ML_DV), F32), jnp.zeros((Bp, ML_HEADS, ML_DK), F32),
           jnp.zeros((Bp, ML_HEADS), F32))
    s50 = (jnp.zeros((Bp, S5_GROUPS, S5_STATE), F32), jnp.zeros((Bp, S5_GROUPS, S5_STATE), F32))
    y_p, (c_p, n_p, m_p, pe_p, sre_p, sim_p, po_p) = _trunk(
        x_prompt.reshape(Bp * S, D), ml0, s50,
        functools.partial(_sb_prompt, sb_bias=sb_bias, n_seq=Bp, seq_len=S),
        functools.partial(_swa_prompt, sinks=sinks, n_seq=Bp, seq_len=S),
        p, n_seq=Bp, T=S, t_real=S)
    ksb0 = 3 * SB_W + SB_W
    sb_k_p = pe_p[:, ksb0:ksb0 + SB_W].reshape(Bp, S, SB_HEADS, SB_DH)
    sb_v_p = pe_p[:, ksb0 + SB_W:].reshape(Bp, S, SB_HEADS, SB_DH)
    po3 = po_p.reshape(Bp, S, -1)
    swa_k_p = po3[:, -WINDOW:, S5_CH + SW_Q:S5_CH + SW_Q + SW_KV].reshape(Bp, WINDOW, SW_KV_HEADS, SW_DH)
    swa_v_p = po3[:, -WINDOW:, S5_CH + SW_Q + SW_KV:].reshape(Bp, WINDOW, SW_KV_HEADS, SW_DH)

    T = T_PAD
    xs = jnp.pad(x_sample, ((0, 0), (0, T - Ts), (0, 0))).reshape(DB * T, D)
    bias_col = jnp.repeat(sb_bias, T).reshape(SB_HEADS * T, 1)
    head_eye = jnp.eye(SB_HEADS, dtype=F32)

    def sb_sample_fn(pe):
        pe3 = pe.reshape(DB, T, -1)
        q = pe3[:, :, 3 * SB_W:4 * SB_W].reshape(DB, T, SB_HEADS, SB_DH)
        qbd = (q.transpose(0, 2, 1, 3)[:, :, :, None, :] * head_eye[None, :, None, :, None]).reshape(
            DB, SB_HEADS * T, SB_W)
        out = _sb_sample(qbd, bias_col, pe3[:, :, 4 * SB_W:5 * SB_W], pe3[:, :, 5 * SB_W:6 * SB_W],
                         cache_sb_k, cache_sb_v, page_table)
        return out.reshape(DB * T, SB_W)

    ck = cache_swa_k.reshape(DB, WINDOW, SW_KV)
    cv = cache_swa_v.reshape(DB, WINDOW, SW_KV)
    y_s, (c_s, n_s, m_s, pe_s, sre_s, sim_s, po_s) = _trunk(
        xs, (state_mlstm_C, state_mlstm_n, state_mlstm_m), (state_s5_re, state_s5_im),
        sb_sample_fn, functools.partial(_swa_sample, sinks=sinks, cache_k=ck, cache_v=cv, n_seq=DB, T=T),
        p, n_seq=DB, T=T, t_real=Ts)
    y_s = y_s.reshape(DB, T, D)[:, :Ts]
    pe_s3 = pe_s.reshape(DB, T, -1)[:, :Ts]
    sb_k_s = pe_s3[:, :, ksb0:ksb0 + SB_W].reshape(DB, Ts, SB_HEADS, SB_DH)
    sb_v_s = pe_s3[:, :, ksb0 + SB_W:].reshape(DB, Ts, SB_HEADS, SB_DH)
    po_s3 = po_s.reshape(DB, T, -1)[:, :Ts]
    k_new = po_s3[:, :, S5_CH + SW_Q:S5_CH + SW_Q + SW_KV].reshape(DB, Ts, SW_KV_HEADS, SW_DH)
    v_new = po_s3[:, :, S5_CH + SW_Q + SW_KV:].reshape(DB, Ts, SW_KV_HEADS, SW_DH)
    swa_k_s = jnp.concatenate([cache_swa_k[:, Ts:], k_new], axis=1)
    swa_v_s = jnp.concatenate([cache_swa_v[:, Ts:], v_new], axis=1)

    return (y_p.reshape(Bp, S, D), y_s, c_p, n_p, m_p, sb_k_p, sb_v_p, sre_p, sim_p, swa_k_p, swa_v_p,
            c_s, n_s, m_s, sb_k_s, sb_v_s, sre_s, sim_s, swa_k_s, swa_v_s)
```

```python
import functools
import math

import jax
import jax.numpy as jnp
import numpy as np
from jax import lax
from jax.experimental import pallas as pl
from jax.experimental.pallas import tpu as pltpu

F32 = jnp.float32
BF16 = jnp.bfloat16

D_MODEL = 1024
PAST_LEN = 8192
ML_HEADS, ML_DK, ML_DV = 4, 64, 128
ML_QK, ML_V = ML_HEADS * ML_DK, ML_HEADS * ML_DV
SB_HEADS, SB_DH = 8, 64
SB_W = SB_HEADS * SB_DH
S5_CH, S5_GROUP, S5_GROUPS, S5_STATE = 512, 16, 32, 64
S5_N = S5_GROUPS * S5_STATE
SW_HEADS, SW_KV_HEADS, SW_DH = 8, 2, 64
SW_Q, SW_KV = SW_HEADS * SW_DH, SW_KV_HEADS * SW_DH
SW_GROUP = SW_HEADS // SW_KV_HEADS
WINDOW = 128
D_FF = 4 * D_MODEL
EPS = 1e-6

LANES = 128
SUBLANES = 8
VMEM_LIMIT = 56 * 1024 * 1024
T_PAD = SUBLANES
ML_CHUNK = 128
SB_TILE = 128
PAGES_PER_STEP = 16
S5_TT = 32


def _params(sem):
    return pltpu.CompilerParams(dimension_semantics=sem, vmem_limit_bytes=VMEM_LIMIT)


def _dot(a, b):
    return jnp.dot(a, b, preferred_element_type=F32)


def _dot_nt(a, b):
    return lax.dot_general(a, b, (((1,), (1,)), ((), ())), preferred_element_type=F32)


def _dot_tn(a, b):
    return lax.dot_general(a, b, (((0,), (0,)), ((), ())), preferred_element_type=F32)


def _rms(x, g):
    ms = jnp.mean(x * x, axis=-1, keepdims=True)
    return x * lax.rsqrt(ms + EPS) * g


def _softplus(z):
    return jnp.maximum(z, 0.0) + jnp.log1p(jnp.exp(-jnp.abs(z)))


def _sigmoid(z):
    return 1.0 / (1.0 + jnp.exp(-z))


def _split_bf16(x):
    hi = x.astype(BF16)
    lo = (x - hi.astype(F32)).astype(BF16)
    return hi, lo


def _inproj_kernel(x_ref, g_ref, w_ref, o_ref):
    o_ref[...] = _dot(_rms(x_ref[...], g_ref[...]).astype(BF16), w_ref[...])


def _inproj(x, g, w):
    M, D = x.shape
    N = w.shape[1]
    tm = min(512, M)
    return pl.pallas_call(
        _inproj_kernel, grid=(M // tm,),
        in_specs=[pl.BlockSpec((tm, D), lambda i: (i, 0)), pl.BlockSpec((1, D), lambda i: (0, 0)),
                  pl.BlockSpec((D, N), lambda i: (0, 0))],
        out_specs=pl.BlockSpec((tm, N), lambda i: (i, 0)),
        out_shape=jax.ShapeDtypeStruct((M, N), F32),
        compiler_params=_params(("parallel",)), name="inproj")(x, g, w)


EVEN_SPLIT = (2 * ML_QK + 2 * ML_V, SB_W, SB_W, SB_W)


def _inproj_even_kernel(x_ref, g_ref, w_ref, wg_ref, gb_ref, pm_ref, qs_ref, ks_ref, vs_ref, gate_ref):
    hn = _rms(x_ref[...], g_ref[...]).astype(BF16)
    res = _dot(hn, w_ref[...])
    off = 0
    for ref, width in zip((pm_ref, qs_ref, ks_ref, vs_ref), EVEN_SPLIT):
        ref[...] = res[:, off:off + width]
        off += width
    gate_ref[...] = _dot(hn, wg_ref[...]) + gb_ref[...]


def _inproj_even(x, g, w, wg, gb):
    M, D = x.shape
    N = w.shape[1]
    tm = min(512, M)
    widths = EVEN_SPLIT + (LANES,)
    return pl.pallas_call(
        _inproj_even_kernel, grid=(M // tm,),
        in_specs=[pl.BlockSpec((tm, D), lambda i: (i, 0)), pl.BlockSpec((1, D), lambda i: (0, 0)),
                  pl.BlockSpec((D, N), lambda i: (0, 0)), pl.BlockSpec((D, LANES), lambda i: (0, 0)),
                  pl.BlockSpec((1, LANES), lambda i: (0, 0))],
        out_specs=[pl.BlockSpec((tm, wd), lambda i: (i, 0)) for wd in widths],
        out_shape=[jax.ShapeDtypeStruct((M, wd), F32) for wd in widths],
        compiler_params=_params(("parallel",)), name="inproj_even")(x, g, w, wg, gb)


def _inproj_tmajor_kernel(x_ref, g_ref, w_ref, o_ref, ut_ref):
    res = _dot(_rms(x_ref[...], g_ref[...]).astype(BF16), w_ref[...])
    o_ref[...] = res
    ut_ref[...] = res[:, :S5_CH]


def _inproj_tmajor(x, g, w, *, n_seq, T):
    M, D = x.shape
    N = w.shape[1]
    tm = 512
    nt = T // tm
    return pl.pallas_call(
        _inproj_tmajor_kernel, grid=(M // tm,),
        in_specs=[pl.BlockSpec((tm, D), lambda i: (i, 0)), pl.BlockSpec((1, D), lambda i: (0, 0)),
                  pl.BlockSpec((D, N), lambda i: (0, 0))],
        out_specs=[pl.BlockSpec((tm, N), lambda i: (i, 0)),
                   pl.BlockSpec((tm, S5_CH), lambda i: (i % nt, i // nt))],
        out_shape=[jax.ShapeDtypeStruct((M, N), F32), jax.ShapeDtypeStruct((T, n_seq * S5_CH), F32)],
        compiler_params=_params(("parallel",)), name="inproj_tmajor")(x, g, w)


def _mlstm_kernel(q_ref, k_ref, v_ref, o_ref, gt_ref, gh_ref, c0_ref, n0_ref, m0_ref,
                  h_ref, c_out, n_out, m_out, caug_scr, m_scr, *, G, L, CS, last, nc):
    c = pl.program_id(1)
    first_chunk = (lambda f: f()) if nc == 1 else pl.when(c == 0)
    last_chunk = (lambda f: f()) if nc == 1 else pl.when(c == nc - 1)
    row = lax.broadcasted_iota(jnp.int32, (L, L), 0)
    col = lax.broadcasted_iota(jnp.int32, (L, L), 1)
    causal = col <= row
    eye = col == row
    e0 = (lax.broadcasted_iota(jnp.int32, (L, LANES), 1) == 0).astype(F32)
    r64 = lax.broadcasted_iota(jnp.int32, (ML_DK, ML_DK), 0)
    c64 = lax.broadcasted_iota(jnp.int32, (ML_DK, ML_DK), 1)
    eye64 = r64 == c64
    lane_dk = lax.broadcasted_iota(jnp.int32, (ML_DK, LANES), 1)
    lane_row = lax.broadcasted_iota(jnp.int32, (1, LANES), 1)
    rowid = lax.broadcasted_iota(jnp.int32, (L, 1), 0)

    def to_row(colvec):
        return jnp.sum(jnp.where(eye, colvec, 0.0), axis=0, keepdims=True)

    def seq_body(g, carry):
        @first_chunk
        def _():
            for h in range(ML_HEADS):
                nrow = n0_ref[g, pl.ds(h, 1), :]
                ncol = jnp.sum(jnp.where(eye64, nrow, 0.0), axis=1, keepdims=True)
                caug_scr[g, h] = jnp.concatenate(
                    [c0_ref[g, h], jnp.where(lane_dk == 0, ncol, 0.0)], axis=1)
                m_scr[g, h] = jnp.broadcast_to(m0_ref[g][:, h:h + 1], (SUBLANES, LANES))

        m_row = jnp.zeros((1, LANES), F32)
        for kk, h in [(kk, h) for kk in range(CS) for h in range(ML_HEADS)]:
            r0 = g * (L * CS) + kk * L
            rows = pl.ds(r0 if isinstance(r0, int) else pl.multiple_of(r0, L), L)
            gates = gt_ref[rows, :]
            q = q_ref[rows, pl.ds(h * ML_DK, ML_DK)]
            kc = k_ref[rows, pl.ds(h * ML_DK, ML_DK)] * (ML_DK ** -0.5)
            v = v_ref[rows, pl.ds(h * ML_DV, ML_DV)]
            icol = gates[:, h:h + 1]
            fcol = gates[:, ML_HEADS + h:ML_HEADS + h + 1]
            caug = caug_scr[g, h]
            m_prev = m_scr[g, h][0:1, 0:1]

            lfcol = jnp.minimum(fcol, 0.0) - jnp.log1p(jnp.exp(-jnp.abs(fcol)))
            lfrow = to_row(lfcol)
            irow = to_row(icol)
            bcol = jnp.sum(jnp.where(causal, lfrow, 0.0), axis=1, keepdims=True)
            brow = to_row(bcol)
            dmat = jnp.where(causal, bcol - brow + irow, -jnp.inf)
            inter = bcol + m_prev
            mt = jnp.maximum(inter, jnp.max(dmat, axis=1, keepdims=True))
            w_inter = jnp.exp(inter - mt)
            qb = q.astype(BF16)
            s = _dot_nt(qb, kc.astype(BF16)) * jnp.exp(dmat - mt)
            vaug = jnp.concatenate([v, e0], axis=1).astype(BF16)
            tot = w_inter * _dot(qb, caug.astype(BF16)) + _dot(s.astype(BF16), vaug)
            num = tot[:, :ML_DV]
            den = tot[:, ML_DV:ML_DV + 1]
            hh = num / jnp.maximum(jnp.abs(den), jnp.exp(-mt))

            m_new = mt[last:last + 1, :]
            g_state = jnp.exp(inter[last:last + 1, :] - m_new)
            gin = jnp.exp(bcol[last:last + 1, :] - bcol + icol - m_new)
            if last < L - 1:
                gin = jnp.where(rowid <= last, gin, 0.0)
            caug_new = g_state * caug + _dot_tn((kc * gin).astype(BF16), vaug)
            caug_scr[g, h] = caug_new
            m_scr[g, h] = jnp.broadcast_to(m_new, (SUBLANES, LANES))
            m_row = jnp.where(lane_row == h, m_new, m_row)

            hh = hh * lax.rsqrt(jnp.mean(hh * hh, axis=-1, keepdims=True) + EPS)
            hh = hh * gh_ref[:, pl.ds(h * ML_DV, ML_DV)]
            hh = hh * _sigmoid(o_ref[rows, pl.ds(h * ML_DV, ML_DV)])
            h_ref[rows, pl.ds(h * ML_DV, ML_DV)] = hh

        @last_chunk
        def _():
            for h in range(ML_HEADS):
                caug_fin = caug_scr[g, h]
                c_out[g, h] = caug_fin[:, :ML_DV]
                ncol = caug_fin[:, ML_DV:ML_DV + 1]
                n_out[g, pl.ds(h, 1), :] = jnp.sum(jnp.where(eye64, ncol, 0.0), axis=0, keepdims=True)
            m_out[g] = m_row

        return carry

    if G == 1:
        seq_body(0, 0)
    else:
        lax.fori_loop(0, G, seq_body, 0, unroll=2)


def _mlstm(pe, gates, g_head, c0, n0, m0, *, n_seq, seq_len, last):
    M = pe.shape[0]
    if seq_len >= ML_CHUNK:
        G, L = 1, ML_CHUNK
        CS = 1
    else:
        L, CS = seq_len, 1
        G = ML_CHUNK // L
    nc = seq_len // (L * CS)
    last_l = last - (seq_len // L - 1) * L
    assert CS == 1 or last_l == L - 1
    grid = (n_seq // G, nc)
    rb = lambda b, c: b * nc + c
    R = G * L * CS
    kernel = functools.partial(_mlstm_kernel, G=G, L=L, CS=CS, last=last_l, nc=nc)
    return pl.pallas_call(
        kernel, grid=grid,
        in_specs=[
            pl.BlockSpec((R, ML_QK), lambda b, c: (rb(b, c), 0)),
            pl.BlockSpec((R, ML_QK), lambda b, c: (rb(b, c), 1)),
            pl.BlockSpec((R, ML_V), lambda b, c: (rb(b, c), 1)),
            pl.BlockSpec((R, ML_V), lambda b, c: (rb(b, c), 2)),
            pl.BlockSpec((R, LANES), lambda b, c: (rb(b, c), 0)),
            pl.BlockSpec((1, ML_V), lambda b, c: (0, 0)),
            pl.BlockSpec((G, ML_HEADS, ML_DK, ML_DV), lambda b, c: (b, 0, 0, 0)),
            pl.BlockSpec((G, ML_HEADS, ML_DK), lambda b, c: (b, 0, 0)),
            pl.BlockSpec((G, 1, ML_HEADS), lambda b, c: (b, 0, 0)),
        ],
        out_specs=[
            pl.BlockSpec((R, ML_V), lambda b, c: (rb(b, c), 0)),
            pl.BlockSpec((G, ML_HEADS, ML_DK, ML_DV), lambda b, c: (b, 0, 0, 0)),
            pl.BlockSpec((G, ML_HEADS, ML_DK), lambda b, c: (b, 0, 0)),
            pl.BlockSpec((G, 1, LANES), lambda b, c: (b, 0, 0)),
        ],
        out_shape=[
            jax.ShapeDtypeStruct((M, ML_V), F32),
            jax.ShapeDtypeStruct((n_seq, ML_HEADS, ML_DK, ML_DV), F32),
            jax.ShapeDtypeStruct((n_seq, ML_HEADS, ML_DK), F32),
            jax.ShapeDtypeStruct((n_seq, 1, LANES), F32),
        ],
        scratch_shapes=[pltpu.VMEM((G, ML_HEADS, ML_DK, 2 * ML_DV), F32),
                        pltpu.VMEM((G, ML_HEADS, SUBLANES, LANES), F32)],
        compiler_params=_params(("parallel", "arbitrary")), name="mlstm",
    )(pe, pe, pe, pe, gates, g_head, c0, n0, m0.reshape(n_seq, 1, ML_HEADS))


def _suffix_matrix(n):
    r = lax.broadcasted_iota(jnp.int32, (n, n), 0)
    c = lax.broadcasted_iota(jnp.int32, (n, n), 1)
    return (r > c).astype(BF16)


def _sb_tile(z, valid, lrem, u_mat):
    sp = _softplus(z)
    lk = -sp if valid is None else jnp.where(valid, -sp, 0.0)
    hi, lo = _split_bf16(lk)
    later = _dot(hi, u_mat) + _dot(lo, u_mat)
    w = jnp.exp(z - sp + later + lrem)
    if valid is not None:
        w = jnp.where(valid, w, 0.0)
    return w, lrem + jnp.sum(lk, axis=1, keepdims=True)


def _suffix_rows(x, carry):
    K, Q = x.shape
    sub = lax.broadcasted_iota(jnp.int32, (SUBLANES, Q), 0)
    keep = {step: (sub < SUBLANES - step).astype(F32) for step in (1, 2, 4)}
    outs = [None] * (K // SUBLANES)
    for j in reversed(range(K // SUBLANES)):
        y = x[j * SUBLANES:(j + 1) * SUBLANES, :]
        for step in (1, 2, 4):
            y = y + keep[step] * pltpu.roll(y, SUBLANES - step, 0)
        outs[j] = y + carry
        carry = carry + jnp.broadcast_to(y[0:1, :], (SUBLANES, Q))
    return jnp.concatenate(outs, axis=0), carry


def _sb_prompt_kernel(bias_ref, q_ref, k_ref, v_ref, o_ref, ks_ref, vs_ref, qt_ref, acc_ref, *stage_refs, T):
    z_refs, w_refs = stage_refs[:SB_HEADS], stage_refs[SB_HEADS:]
    _sb_prompt_body(bias_ref, q_ref, k_ref, v_ref, o_ref, ks_ref, vs_ref, qt_ref, acc_ref, z_refs, w_refs, T)


def _sb_prompt_body(bias_ref, q_ref, k_ref, v_ref, o_ref, ks_ref, vs_ref, qt_ref, acc_ref, z_refs, w_refs, T):
    qi = pl.program_id(1)
    lane = lax.broadcasted_iota(jnp.int32, (1, LANES), 1)

    @pl.when(qi == 0)
    def _():
        for h in range(SB_HEADS):
            vs_ref[h] = v_ref[:, pl.ds(h * SB_DH, SB_DH)].astype(BF16)
        for hp in range(SB_HEADS // 2):
            pair = k_ref[:, pl.ds(hp * LANES, LANES)]
            ks_ref[2 * hp] = jnp.where(lane < SB_DH, pair, (lane < SB_DH + 2).astype(F32)).astype(BF16)
            ks_ref[2 * hp + 1] = jnp.where(lane >= SB_DH, pair, (lane < 2).astype(F32)).astype(BF16)

    rowi = lax.broadcasted_iota(jnp.int32, (LANES, T), 0)
    for hp in range(SB_HEADS // 2):
        qt = (q_ref[:, pl.ds(hp * LANES, LANES)] * (SB_DH ** -0.5)).T
        for h, own, r0 in ((2 * hp, rowi < SB_DH, SB_DH), (2 * hp + 1, rowi >= SB_DH, 0)):
            b = jnp.full((LANES, T), bias_ref[h], F32)
            b_hi = b.astype(BF16).astype(F32)
            aug = jnp.where(rowi == r0, b_hi, jnp.where(rowi == r0 + 1, b - b_hi, 0.0))
            qt_ref[h] = jnp.where(own, qt, aug).astype(BF16)

    row = lax.broadcasted_iota(jnp.int32, (T, T), 0)
    col = lax.broadcasted_iota(jnp.int32, (T, T), 1)
    strict = row < col

    def keys(j):
        return pl.ds(pl.multiple_of(jnp.maximum(j, 0) * T, T), T)

    def logits(h, j):
        return _dot(ks_ref[h, keys(j), :], qt_ref[h])

    def weights(zt, lrem, valid):
        sp = jnp.maximum(zt, 0.0) + jnp.log(1.0 + jnp.exp(-jnp.abs(zt)))
        if valid is not None:
            sp = jnp.where(valid, sp, 0.0)
        y, lrem = _suffix_rows(sp, lrem)
        w = jnp.exp(zt - y)
        if valid is not None:
            w = jnp.where(valid, w, 0.0)
        return w.astype(BF16), lrem

    def values(h, j, w):
        return _dot_tn(w, vs_ref[h, keys(j), :])

    lrems = []
    for h in range(SB_HEADS):
        w, lrem = weights(logits(h, qi), jnp.zeros((SUBLANES, T), F32), strict)
        acc_ref[h] = jnp.zeros((T, SB_DH), F32)
        lrems.append(lrem)
        w_refs[h][1] = w
        z_refs[h][0] = logits(h, qi - 1)

    def body(it, lrems):
        j = qi - 1 - it
        cur = lax.rem(it, 2)
        nxt = 1 - cur
        out = []
        for h in range(SB_HEADS):
            acc_ref[h] += values(h, j + 1, w_refs[h][nxt])
            w, lrem = weights(z_refs[h][cur], lrems[h], None)
            w_refs[h][cur] = w
            z_refs[h][nxt] = logits(h, j - 1)
            out.append(lrem)
        return tuple(out)

    lax.fori_loop(0, qi, body, tuple(lrems))
    last = lax.rem(qi + 1, 2)
    o_ref[...] = jnp.concatenate(
        [acc_ref[h] + values(h, 0, w_refs[h][last]) for h in range(SB_HEADS)], axis=1)


def _sb_prompt(q, k, v, sb_bias, *, n_seq, seq_len):
    M = q.shape[0]
    T = SB_TILE
    nq = seq_len // T
    kernel = functools.partial(_sb_prompt_kernel, T=T)
    return pl.pallas_call(
        kernel, grid=(n_seq, nq),
        in_specs=[
            pl.BlockSpec(memory_space=pltpu.SMEM),
            pl.BlockSpec((T, SB_W), lambda b, i: (b * nq + i, 0)),
            pl.BlockSpec((seq_len, SB_W), lambda b, i: (b, 0)),
            pl.BlockSpec((seq_len, SB_W), lambda b, i: (b, 0)),
        ],
        out_specs=pl.BlockSpec((T, SB_W), lambda b, i: (b * nq + i, 0)),
        out_shape=jax.ShapeDtypeStruct((M, SB_W), F32),
        scratch_shapes=[pltpu.VMEM((SB_HEADS, seq_len, LANES), BF16),
                        pltpu.VMEM((SB_HEADS, seq_len, SB_DH), BF16),
                        pltpu.VMEM((SB_HEADS, LANES, T), BF16),
                        pltpu.VMEM((SB_HEADS, T, SB_DH), F32)]
        + [pltpu.VMEM((2, T, T), F32)] * SB_HEADS + [pltpu.VMEM((2, T, T), BF16)] * SB_HEADS,
        compiler_params=_params(("parallel", "arbitrary")), name="sb_prompt",
    )(sb_bias, q, k, v)


def _sb_sample_kernel(pt_ref, bias_ref, qbd_ref, kn_ref, vn_ref, *refs, P, T):
    k_refs, v_refs = refs[:P], refs[P:2 * P]
    o_ref, acc_ref, lrem_ref = refs[2 * P], refs[2 * P + 1], refs[2 * P + 2]
    s = pl.program_id(1)
    R = SB_HEADS * T
    u_mat = _suffix_matrix(LANES)
    qb = (qbd_ref[0] * (SB_DH ** -0.5)).astype(BF16)
    bias = bias_ref[...]

    @pl.when(s == 0)
    def _():
        pad = jnp.zeros((LANES - T, SB_W), F32)
        kn = jnp.concatenate([kn_ref[0], pad], axis=0).astype(BF16)
        vn = jnp.concatenate([vn_ref[0], pad], axis=0).astype(BF16)
        t_of_row = lax.broadcasted_iota(jnp.int32, (R, LANES), 0) % T
        colk = lax.broadcasted_iota(jnp.int32, (R, LANES), 1)
        z = _dot_nt(qb, kn) + bias
        w, lrem = _sb_tile(z, colk < t_of_row, jnp.zeros((R, 1), F32), u_mat)
        acc_ref[...] = _dot(w.astype(BF16), vn)
        lrem_ref[...] = jnp.broadcast_to(lrem, (R, LANES))

    acc = acc_ref[...]
    lrem = lrem_ref[:, 0:1]
    kt = jnp.concatenate([k_refs[i][0].reshape(SB_W, LANES) for i in range(P)], axis=1).astype(BF16)
    z_all = _dot(qb, kt)
    z = jnp.concatenate([z_all[:, i * LANES:(i + 1) * LANES] for i in range(P)], axis=0)
    z = z + jnp.concatenate([bias] * P, axis=0)
    sp = _softplus(z)
    hi, lo = _split_bf16(-sp)
    later = _dot(hi, u_mat) + _dot(lo, u_mat)
    tot = jnp.sum(sp, axis=1, keepdims=True)
    lrems = []
    for i in range(P):
        lrems.append(lrem)
        lrem = lrem - tot[i * R:(i + 1) * R]
    w = jnp.exp(z - sp + later + jnp.concatenate(lrems, axis=0)).astype(BF16)
    w_all = jnp.concatenate([w[i * R:(i + 1) * R] for i in range(P)], axis=1)
    v = jnp.concatenate([v_refs[i][0].reshape(SB_W, LANES).T for i in range(P)], axis=0).astype(BF16)
    acc = acc + _dot(w_all, v)
    acc_ref[...] = acc
    lrem_ref[...] = jnp.broadcast_to(lrem, (R, LANES))

    @pl.when(s == pl.num_programs(1) - 1)
    def _():
        o_ref[0] = jnp.concatenate(
            [acc[h * T:(h + 1) * T, h * SB_DH:(h + 1) * SB_DH] for h in range(SB_HEADS)], axis=1)


def _sb_sample(qbd, bias_col, k_new, v_new, cache_k, cache_v, page_table):
    DB, R, _ = qbd.shape
    T = R // SB_HEADS
    n_pages = page_table.shape[1]
    P = PAGES_PER_STEP
    page = cache_k.shape[1]
    assert page == LANES
    ck = cache_k.transpose(0, 2, 3, 1)
    cv = cache_v.transpose(0, 2, 3, 1)

    def page_spec(i):
        return pl.BlockSpec((1, SB_HEADS, SB_DH, page),
                            lambda b, s, pt: (pt[b, n_pages - 1 - (s * P + i)], 0, 0, 0))

    grid_spec = pltpu.PrefetchScalarGridSpec(
        num_scalar_prefetch=1, grid=(DB, n_pages // P),
        in_specs=[pl.BlockSpec((R, 1), lambda b, s, pt: (0, 0)),
                  pl.BlockSpec((1, R, SB_W), lambda b, s, pt: (b, 0, 0)),
                  pl.BlockSpec((1, T, SB_W), lambda b, s, pt: (b, 0, 0)),
                  pl.BlockSpec((1, T, SB_W), lambda b, s, pt: (b, 0, 0))]
        + [page_spec(i) for i in range(P)] + [page_spec(i) for i in range(P)],
        out_specs=pl.BlockSpec((1, T, SB_W), lambda b, s, pt: (b, 0, 0)),
        scratch_shapes=[pltpu.VMEM((R, SB_W), F32), pltpu.VMEM((R, LANES), F32)])
    kernel = functools.partial(_sb_sample_kernel, P=P, T=T)
    return pl.pallas_call(
        kernel, grid_spec=grid_spec, out_shape=jax.ShapeDtypeStruct((DB, T, SB_W), F32),
        compiler_params=_params(("parallel", "arbitrary")), name="sb_sample",
    )(page_table, bias_col, qbd, k_new, v_new, *([ck] * P), *([cv] * P))


def _s5_param_kernel(lre_ref, lim_ref, ldt_ref, bre_ref, bim_ref, are_ref, aim_ref, bbre_ref, bbim_ref):
    lre = jnp.minimum(lre_ref[...], -1e-4)
    lim = lim_ref[...]
    dt = jnp.exp(ldt_ref[...])
    mag = jnp.exp(lre * dt)
    ab_re = mag * jnp.cos(lim * dt)
    ab_im = mag * jnp.sin(lim * dt)
    den = lre * lre + lim * lim
    c_re = ((ab_re - 1.0) * lre + ab_im * lim) / den
    c_im = (ab_im * lre - (ab_re - 1.0) * lim) / den
    b_re, b_im = bre_ref[...], bim_ref[...]
    are_ref[...] = ab_re
    aim_ref[...] = ab_im
    bbre_ref[...] = c_re * b_re - c_im * b_im
    bbim_ref[...] = c_re * b_im + c_im * b_re


def _s5_params(lam_re, lam_im, log_dt, b_re, b_im):
    N = S5_N
    col = lambda a: a.reshape(N, 1)
    ldt = jnp.repeat(log_dt, S5_STATE).reshape(N, 1)
    full = lambda shape: pl.BlockSpec(shape, lambda: (0,) * len(shape))
    are, aim, bbre, bbim = pl.pallas_call(
        _s5_param_kernel,
        in_specs=[full((N, 1))] * 3 + [full((N, S5_GROUP))] * 2,
        out_specs=[full((N, 1))] * 2 + [full((N, S5_GROUP))] * 2,
        out_shape=[jax.ShapeDtypeStruct((N, 1), F32)] * 2 + [jax.ShapeDtypeStruct((N, S5_GROUP), F32)] * 2,
        name="s5_params",
    )(col(lam_re), col(lam_im), ldt, b_re.reshape(N, S5_GROUP), b_im.reshape(N, S5_GROUP))
    return are.reshape(1, N), aim.reshape(1, N), bbre, bbim


def _block_diag(blocks):
    G, r, c = blocks.shape
    eye = jnp.eye(G, dtype=blocks.dtype)
    return (blocks[:, :, None, :] * eye[:, None, :, None]).reshape(G * r, G * c)


def _s5_kernel(u_ref, are_ref, aim_ref, bb_ref, cre_ref, cim_ref, d_ref, wg_ref, bg_ref, sre0_ref, sim0_ref,
               y_ref, sre_out, sim_out, bu_scr, xs_scr, st_scr, *, TT, t_last):
    tb = pl.program_id(1)
    N = S5_N
    CH = 512
    BS = SUBLANES

    @pl.when(tb == 0)
    def _():
        st_scr[:, 0:N] = sre0_ref[...]
        st_scr[:, N:2 * N] = sim0_ref[...]

    u = u_ref[...].reshape(TT * BS, S5_CH)
    ub = u.astype(BF16)
    UH, NH = S5_CH // 2, N // 2
    for hf in range(2):
        res = _dot(ub[:, hf * UH:(hf + 1) * UH], bb_ref[hf])
        bu_scr[:, hf * NH:(hf + 1) * NH] = res[:, :NH]
        bu_scr[:, N + hf * NH:N + (hf + 1) * NH] = res[:, NH:]

    for cidx in range(N // CH):
        lo = cidx * CH
        ar = jnp.broadcast_to(are_ref[:, lo:lo + CH], (BS, CH))
        ai = jnp.broadcast_to(aim_ref[:, lo:lo + CH], (BS, CH))

        def step(t, carry):
            xr, xi = carry
            r = pl.ds(pl.multiple_of(t * BS, BS), BS)
            br = bu_scr[r, lo:lo + CH]
            bi = bu_scr[r, N + lo:N + lo + CH]
            nr = ar * xr - ai * xi + br
            ni = ar * xi + ai * xr + bi
            xs_scr[r, lo:lo + CH] = nr
            xs_scr[r, N + lo:N + lo + CH] = ni
            return nr, ni

        xr, xi = lax.fori_loop(0, TT, step, (st_scr[:, lo:lo + CH], st_scr[:, N + lo:N + lo + CH]))
        st_scr[:, lo:lo + CH] = xr
        st_scr[:, N + lo:N + lo + CH] = xi

    @pl.when(tb == t_last // TT)
    def _():
        r = pl.ds((t_last % TT) * BS, BS)
        sre_out[...] = xs_scr[r, 0:N]
        sim_out[...] = xs_scr[r, N:2 * N]

    ys = []
    for hf in range(2):
        x_re = xs_scr[:, hf * NH:(hf + 1) * NH].astype(BF16)
        x_im = xs_scr[:, N + hf * NH:N + (hf + 1) * NH].astype(BF16)
        ys.append(_dot(x_re, cre_ref[hf]) - _dot(x_im, cim_ref[hf]))
    y = jnp.concatenate(ys, axis=1) + d_ref[...] * u
    y = jax.nn.gelu(y)
    y = y * _sigmoid(_dot(y.astype(BF16), wg_ref[...]) + bg_ref[...])
    y_ref[...] = y.reshape(TT, BS, S5_CH)


def _s5(u_t, are, aim, bb, cre, cim, d, wg, bg, sre0, sim0, *, t_last):
    T, B, _ = u_t.shape
    TT = min(S5_TT, T)
    N = S5_N
    const = lambda shape: pl.BlockSpec(shape, lambda b, t: (0,) * len(shape))
    kernel = functools.partial(_s5_kernel, TT=TT, t_last=t_last)
    return pl.pallas_call(
        kernel, grid=(B // SUBLANES, T // TT),
        in_specs=[pl.BlockSpec((TT, SUBLANES, S5_CH), lambda b, t: (t, b, 0)),
                  const((1, N)), const((1, N)), const((2, S5_CH // 2, N)), const((2, N // 2, S5_CH // 2)),
                  const((2, N // 2, S5_CH // 2)),
                  const((1, S5_CH)), const((S5_CH, S5_CH)), const((1, S5_CH)),
                  pl.BlockSpec((SUBLANES, N), lambda b, t: (b, 0)),
                  pl.BlockSpec((SUBLANES, N), lambda b, t: (b, 0))],
        out_specs=[pl.BlockSpec((TT, SUBLANES, S5_CH), lambda b, t: (t, b, 0)),
                   pl.BlockSpec((SUBLANES, N), lambda b, t: (b, 0)),
                   pl.BlockSpec((SUBLANES, N), lambda b, t: (b, 0))],
        out_shape=[jax.ShapeDtypeStruct((T, B, S5_CH), F32),
                   jax.ShapeDtypeStruct((B, N), F32), jax.ShapeDtypeStruct((B, N), F32)],
        scratch_shapes=[pltpu.VMEM((TT * SUBLANES, 2 * N), F32), pltpu.VMEM((TT * SUBLANES, 2 * N), F32),
                        pltpu.VMEM((SUBLANES, 2 * N), F32)],
        compiler_params=_params(("parallel", "arbitrary")), name="s5",
    )(u_t, are, aim, bb, cre, cim, d, wg, bg, sre0, sim0)


def _alibi_slope(h):
    return float(2.0 ** (-8.0 * (h + 1) / SW_HEADS))


def _sink_softmax_parts(zs, sink):
    m = sink
    for z in zs:
        m = jnp.maximum(m, jnp.max(z, axis=-1, keepdims=True))
    es = [jnp.exp(z - m) for z in zs]
    tot = jnp.exp(sink - m)
    for e in es:
        tot = tot + jnp.sum(e, axis=-1, keepdims=True)
    return [e / tot for e in es]


def _swa_prompt_kernel(sink_ref, q_ref, kp_ref, kc_ref, vp_ref, vc_ref, o_ref):
    i = pl.program_id(1)
    W = WINDOW
    rowq = lax.broadcasted_iota(jnp.int32, (W, 2 * W), 0)
    colc = lax.broadcasted_iota(jnp.int32, (W, 2 * W), 1)
    dist = W + rowq - colc
    valid = (dist >= 0) & (dist < W) & (colc >= jnp.where(i > 0, 0, W))
    distf = dist.astype(F32)
    outs = []
    for kv in range(SW_KV_HEADS):
        ls = pl.ds(kv * SW_DH, SW_DH)
        kband = jnp.concatenate([kp_ref[:, ls], kc_ref[:, ls]], axis=0).astype(BF16)
        vband = jnp.concatenate([vp_ref[:, ls], vc_ref[:, ls]], axis=0).astype(BF16)
        for gq in range(SW_GROUP):
            h = kv * SW_GROUP + gq
            qh = q_ref[:, pl.ds(h * SW_DH, SW_DH)].astype(BF16)
            z = _dot_nt(qh, kband) * (SW_DH ** -0.5) - _alibi_slope(h) * distf
            z = jnp.where(valid, z, -jnp.inf)
            (p,) = _sink_softmax_parts([z], sink_ref[h])
            outs.append(_dot(p.astype(BF16), vband))
    o_ref[...] = jnp.concatenate(outs, axis=1)


def _swa_prompt(po, sinks, *, n_seq, seq_len):
    M = po.shape[0]
    W = WINDOW
    nb = seq_len // W
    kcol, vcol = (S5_CH + SW_Q) // SW_KV, (S5_CH + SW_Q) // SW_KV + 1
    cur = lambda b, i: b * nb + i
    prev = lambda b, i: b * nb + jnp.maximum(i - 1, 0)
    return pl.pallas_call(
        _swa_prompt_kernel, grid=(n_seq, nb),
        in_specs=[pl.BlockSpec(memory_space=pltpu.SMEM),
                  pl.BlockSpec((W, SW_Q), lambda b, i: (cur(b, i), 1)),
                  pl.BlockSpec((W, SW_KV), lambda b, i: (prev(b, i), kcol)),
                  pl.BlockSpec((W, SW_KV), lambda b, i: (cur(b, i), kcol)),
                  pl.BlockSpec((W, SW_KV), lambda b, i: (prev(b, i), vcol)),
                  pl.BlockSpec((W, SW_KV), lambda b, i: (cur(b, i), vcol))],
        out_specs=pl.BlockSpec((W, SW_Q), lambda b, i: (cur(b, i), 0)),
        out_shape=jax.ShapeDtypeStruct((M, SW_Q), F32),
        compiler_params=_params(("parallel", "arbitrary")), name="swa_prompt",
    )(sinks, po, po, po, po, po)


def _swa_sample_kernel(sink_ref, q_ref, kn_ref, vn_ref, ck_ref, cv_ref, o_ref, *, G, T):
    W = WINDOW
    R = SW_GROUP * T
    t_c = lax.broadcasted_iota(jnp.int32, (R, W), 0) % T
    j_c = lax.broadcasted_iota(jnp.int32, (R, W), 1)
    dist_c = W + t_c - j_c
    valid_c = dist_c < W
    t_n = lax.broadcasted_iota(jnp.int32, (R, T), 0) % T
    s_n = lax.broadcasted_iota(jnp.int32, (R, T), 1)
    dist_n = t_n - s_n
    valid_n = dist_n >= 0
    g_of_row = lax.broadcasted_iota(jnp.int32, (R, 1), 0) // T

    def seq_body(g, carry):
        rows = pl.ds(pl.multiple_of(g * T, T), T)
        outs = []
        for kv in range(SW_KV_HEADS):
            ls = pl.ds(kv * SW_DH, SW_DH)
            slope = jnp.zeros((R, 1), F32)
            sink = jnp.zeros((R, 1), F32)
            for gq in range(SW_GROUP):
                h = kv * SW_GROUP + gq
                slope = jnp.where(g_of_row == gq, _alibi_slope(h), slope)
                sink = jnp.where(g_of_row == gq, sink_ref[h], sink)
            q4 = jnp.concatenate(
                [q_ref[rows, pl.ds((kv * SW_GROUP + gq) * SW_DH, SW_DH)] for gq in range(SW_GROUP)],
                axis=0).astype(BF16)
            kc = ck_ref[g, :, ls].astype(BF16)
            vc = cv_ref[g, :, ls].astype(BF16)
            kn = kn_ref[rows, ls].astype(BF16)
            vn = vn_ref[rows, ls].astype(BF16)
            zc = _dot_nt(q4, kc) * (SW_DH ** -0.5) - slope * dist_c.astype(F32)
            zn = _dot_nt(q4, kn) * (SW_DH ** -0.5) - slope * dist_n.astype(F32)
            zc = jnp.where(valid_c, zc, -jnp.inf)
            zn = jnp.where(valid_n, zn, -jnp.inf)
            pc, pn = _sink_softmax_parts([zc, zn], sink)
            o4 = _dot(pc.astype(BF16), vc) + _dot(pn.astype(BF16), vn)
            outs += [o4[gq * T:(gq + 1) * T] for gq in range(SW_GROUP)]
        o_ref[rows, :] = jnp.concatenate(outs, axis=1)
        return carry

    lax.fori_loop(0, G, seq_body, 0, unroll=2)


def _swa_sample(po, sinks, cache_k, cache_v, *, n_seq, T):
    M = po.shape[0]
    G = 16
    kcol, vcol = (S5_CH + SW_Q) // SW_KV, (S5_CH + SW_Q) // SW_KV + 1
    kernel = functools.partial(_swa_sample_kernel, G=G, T=T)
    return pl.pallas_call(
        kernel, grid=(n_seq // G,),
        in_specs=[pl.BlockSpec(memory_space=pltpu.SMEM),
                  pl.BlockSpec((G * T, SW_Q), lambda b: (b, 1)),
                  pl.BlockSpec((G * T, SW_KV), lambda b: (b, kcol)),
                  pl.BlockSpec((G * T, SW_KV), lambda b: (b, vcol)),
                  pl.BlockSpec((G, WINDOW, SW_KV), lambda b: (b, 0, 0)),
                  pl.BlockSpec((G, WINDOW, SW_KV), lambda b: (b, 0, 0))],
        out_specs=pl.BlockSpec((G * T, SW_Q), lambda b: (b, 0)),
        out_shape=jax.ShapeDtypeStruct((M, SW_Q), F32),
        compiler_params=_params(("parallel",)), name="swa_sample",
    )(sinks, po, po, po, cache_k, cache_v)


def _post_kernel(x_ref, a_ref, b_ref, wa_ref, wb_ref, g_ref, w1_ref, w2_ref, gf_ref, o_ref,
                 x1_scr, hn_scr, acc_scr, *, final):
    j = pl.program_id(1)

    @pl.when(j == 0)
    def _():
        x1 = x_ref[...] + _dot(a_ref[...].astype(BF16), wa_ref[...]) + _dot(b_ref[...].astype(BF16), wb_ref[...])
        x1_scr[...] = x1
        hn_scr[...] = _rms(x1, g_ref[...]).astype(BF16)
        acc_scr[...] = jnp.zeros_like(acc_scr)

    hmid = jnp.square(jnp.maximum(_dot(hn_scr[...], w1_ref[...]), 0.0))
    acc_scr[...] += _dot(hmid.astype(BF16), w2_ref[...])

    @pl.when(j == pl.num_programs(1) - 1)
    def _():
        y = x1_scr[...] + acc_scr[...]
        if final:
            y = _rms(y, gf_ref[...])
        o_ref[...] = y


def _post(x, a, b, wa, wb, g, w1, w2, gf, *, final, a_tmajor_seq_len=None):
    M, D = x.shape
    tm = min(512, M)
    tf = 2048
    Ka, Kb = wa.shape[0], wb.shape[0]
    if a_tmajor_seq_len is None:
        a_map = lambda i, j: (i, 0)
    else:
        nt = a_tmajor_seq_len // tm
        a_map = lambda i, j: (i % nt, i // nt)
    kernel = functools.partial(_post_kernel, final=final)
    return pl.pallas_call(
        kernel, grid=(M // tm, D_FF // tf),
        in_specs=[pl.BlockSpec((tm, D), lambda i, j: (i, 0)),
                  pl.BlockSpec((tm, Ka), a_map),
                  pl.BlockSpec((tm, Kb), lambda i, j: (i, 0)),
                  pl.BlockSpec((Ka, D), lambda i, j: (0, 0)),
                  pl.BlockSpec((Kb, D), lambda i, j: (0, 0)),
                  pl.BlockSpec((1, D), lambda i, j: (0, 0)),
                  pl.BlockSpec((D, tf), lambda i, j: (0, j)),
                  pl.BlockSpec((tf, D), lambda i, j: (j, 0)),
                  pl.BlockSpec((1, D), lambda i, j: (0, 0))],
        out_specs=pl.BlockSpec((tm, D), lambda i, j: (i, 0)),
        out_shape=jax.ShapeDtypeStruct((M, D), F32),
        scratch_shapes=[pltpu.VMEM((tm, D), F32), pltpu.VMEM((tm, D), BF16), pltpu.VMEM((tm, D), F32)],
        compiler_params=_params(("parallel", "arbitrary")), name="post",
    )(x, a, b, wa, wb, g, w1, w2, gf)


def _trunk(x, ml_state, s5_state, sb_fn, swa_fn, p, *, n_seq, T, t_real):
    last = t_real - 1
    pm, q_sb, k_sb, v_sb, gates = _inproj_even(x, p["g_mix0"], p["w_even"], p["w_gate"], p["b_gate"])
    h_ml, c_new, n_new, m_new = _mlstm(pm, gates, p["g_head"], *ml_state, n_seq=n_seq, seq_len=T, last=last)
    h_sb = sb_fn(q_sb, k_sb, v_sb)
    x = _post(x, h_ml, h_sb, p["wo_even_a"], p["wo_even_b"], p["g_ffn0"], p["w1_0"], p["w2_0"],
              p["g_final"], final=False)
    tmajor = T % 512 == 0
    if tmajor:
        po, u_t = _inproj_tmajor(x, p["g_mix1"], p["w_odd"], n_seq=n_seq, T=T)
        u_t = u_t.reshape(T, n_seq, S5_CH)
    else:
        po = _inproj(x, p["g_mix1"], p["w_odd"])
        u_t = po[:, :S5_CH].reshape(n_seq, T, S5_CH).transpose(1, 0, 2)
    y_t, s_re, s_im = _s5(u_t, p["a_re"], p["a_im"], p["bb"], p["cc_re"], p["cc_im"], p["s5_d"], p["w_glu"],
                          p["b_glu"], s5_state[0].reshape(n_seq, S5_N), s5_state[1].reshape(n_seq, S5_N),
                          t_last=last)
    o_sw = swa_fn(po)
    if tmajor:
        y_s5, seq_len = y_t.reshape(T, n_seq * S5_CH), T
    else:
        y_s5, seq_len = y_t.transpose(1, 0, 2).reshape(n_seq * T, S5_CH), None
    x = _post(x, y_s5, o_sw, p["wo_odd_a"], p["wo_odd_b"], p["g_ffn1"], p["w1_1"], p["w2_1"],
              p["g_final"], final=True, a_tmajor_seq_len=seq_len)
    state = (c_new, n_new, m_new[:, 0, :ML_HEADS], k_sb, v_sb, s_re.reshape(n_seq, S5_GROUPS, S5_STATE),
             s_im.reshape(n_seq, S5_GROUPS, S5_STATE), po)
    return x, state


def kernel(x_prompt, x_sample, state_mlstm_C, state_mlstm_n, state_mlstm_m, cache_sb_k, cache_sb_v, page_table, state_s5_re, state_s5_im, cache_swa_k, cache_swa_v, g_norm_mix, g_norm_ffn, g_norm_final, w_in_even, b_igate, b_fgate, g_mlstm_head, sb_bias, w_out_even, w_in_odd, s5_lambda_re, s5_lambda_im, s5_log_dt, s5_B_re, s5_B_im, s5_C_re, s5_C_im, s5_D, w_glu, b_glu, swa_sinks, w_out_odd, w_ff1, w_ff2):
    Bp, S, D = x_prompt.shape
    DB, Ts, _ = x_sample.shape
    n_gate = 2 * ML_HEADS
    g0 = 2 * ML_QK + 2 * ML_V
    row = lambda a: a.reshape(1, -1).astype(F32)

    a_re, a_im, bb_re, bb_im = _s5_params(s5_lambda_re, s5_lambda_im, s5_log_dt, s5_B_re, s5_B_im)
    bbt = lambda bb: _block_diag(bb.reshape(S5_GROUPS, S5_STATE, S5_GROUP).transpose(0, 2, 1))
    half_blocks = lambda m, hf: m[hf * m.shape[0] // 2:(hf + 1) * m.shape[0] // 2,
                                  hf * m.shape[1] // 2:(hf + 1) * m.shape[1] // 2]
    p = {
        "g_mix0": row(g_norm_mix[0]), "g_mix1": row(g_norm_mix[1]),
        "g_ffn0": row(g_norm_ffn[0]), "g_ffn1": row(g_norm_ffn[1]), "g_final": row(g_norm_final),
        "w_even": jnp.concatenate([w_in_even[:, :g0], w_in_even[:, g0 + n_gate:]], axis=1).astype(BF16),
        "w_gate": jnp.pad(w_in_even[:, g0:g0 + n_gate], ((0, 0), (0, LANES - n_gate))).astype(BF16),
        "b_gate": jnp.pad(jnp.concatenate([b_igate, b_fgate]), (0, LANES - n_gate)).reshape(1, LANES).astype(F32),
        "g_head": row(g_mlstm_head),
        "wo_even_a": w_out_even[:ML_V].astype(BF16), "wo_even_b": w_out_even[ML_V:].astype(BF16),
        "w_odd": w_in_odd.astype(BF16),
        "a_re": a_re, "a_im": a_im,
        "bb": jnp.stack([jnp.concatenate([half_blocks(bbt(bb_re), hf), half_blocks(bbt(bb_im), hf)], axis=1)
                         for hf in range(2)]).astype(BF16),
        "cc_re": jnp.stack([half_blocks(_block_diag(s5_C_re.transpose(0, 2, 1)), hf)
                            for hf in range(2)]).astype(BF16),
        "cc_im": jnp.stack([half_blocks(_block_diag(s5_C_im.transpose(0, 2, 1)), hf)
                            for hf in range(2)]).astype(BF16),
        "s5_d": row(s5_D), "w_glu": w_glu.astype(BF16), "b_glu": row(b_glu),
        "wo_odd_a": w_out_odd[:S5_CH].astype(BF16), "wo_odd_b": w_out_odd[S5_CH:].astype(BF16),
        "w1_0": w_ff1[0].astype(BF16), "w2_0": w_ff2[0].astype(BF16),
        "w1_1": w_ff1[1].astype(BF16), "w2_1": w_ff2[1].astype(BF16),
    }
    sb_bias = sb_bias.astype(F32)
    sinks = swa_sinks.astype(F32)

    ml0 = (jnp.zeros((Bp, ML_HEADS, ML_DK, ML_DV), F32), jnp.zeros((Bp, ML_HEADS, ML_DK), F32),
           jnp.zeros((Bp, ML_HEADS), F32))
    s50 = (jnp.zeros((Bp, S5_GROUPS, S5_STATE), F32), jnp.zeros((Bp, S5_GROUPS, S5_STATE), F32))
    y_p, (c_p, n_p, m_p, ksb_p, vsb_p, sre_p, sim_p, po_p) = _trunk(
        x_prompt.reshape(Bp * S, D), ml0, s50,
        functools.partial(_sb_prompt, sb_bias=sb_bias, n_seq=Bp, seq_len=S),
        functools.partial(_swa_prompt, sinks=sinks, n_seq=Bp, seq_len=S),
        p, n_seq=Bp, T=S, t_real=S)
    sb_k_p = ksb_p.reshape(Bp, S, SB_HEADS, SB_DH)
    sb_v_p = vsb_p.reshape(Bp, S, SB_HEADS, SB_DH)
    po3 = po_p.reshape(Bp, S, -1)
    swa_k_p = po3[:, -WINDOW:, S5_CH + SW_Q:S5_CH + SW_Q + SW_KV].reshape(Bp, WINDOW, SW_KV_HEADS, SW_DH)
    swa_v_p = po3[:, -WINDOW:, S5_CH + SW_Q + SW_KV:].reshape(Bp, WINDOW, SW_KV_HEADS, SW_DH)

    T = T_PAD
    xs = jnp.pad(x_sample, ((0, 0), (0, T - Ts), (0, 0))).reshape(DB * T, D)
    bias_col = jnp.repeat(sb_bias, T).reshape(SB_HEADS * T, 1)
    head_eye = jnp.eye(SB_HEADS, dtype=F32)

    def sb_sample_fn(q_sb, k_sb, v_sb):
        q = q_sb.reshape(DB, T, SB_HEADS, SB_DH)
        qbd = (q.transpose(0, 2, 1, 3)[:, :, :, None, :] * head_eye[None, :, None, :, None]).reshape(
            DB, SB_HEADS * T, SB_W)
        out = _sb_sample(qbd, bias_col, k_sb.reshape(DB, T, SB_W), v_sb.reshape(DB, T, SB_W),
                         cache_sb_k, cache_sb_v, page_table)
        return out.reshape(DB * T, SB_W)

    ck = cache_swa_k.reshape(DB, WINDOW, SW_KV)
    cv = cache_swa_v.reshape(DB, WINDOW, SW_KV)
    y_s, (c_s, n_s, m_s, ksb_s, vsb_s, sre_s, sim_s, po_s) = _trunk(
        xs, (state_mlstm_C, state_mlstm_n, state_mlstm_m), (state_s5_re, state_s5_im),
        sb_sample_fn, functools.partial(_swa_sample, sinks=sinks, cache_k=ck, cache_v=cv, n_seq=DB, T=T),
        p, n_seq=DB, T=T, t_real=Ts)
    y_s = y_s.reshape(DB, T, D)[:, :Ts]
    sb_k_s = ksb_s.reshape(DB, T, SB_HEADS, SB_DH)[:, :Ts]
    sb_v_s = vsb_s.reshape(DB, T, SB_HEADS, SB_DH)[:, :Ts]
    po_s3 = po_s.reshape(DB, T, -1)[:, :Ts]
    k_new = po_s3[:, :, S5_CH + SW_Q:S5_CH + SW_Q + SW_KV].reshape(DB, Ts, SW_KV_HEADS, SW_DH)
    v_new = po_s3[:, :, S5_CH + SW_Q + SW_KV:].reshape(DB, Ts, SW_KV_HEADS, SW_DH)
    swa_k_s = jnp.concatenate([cache_swa_k[:, Ts:], k_new], axis=1)
    swa_v_s = jnp.concatenate([cache_swa_v[:, Ts:], v_new], axis=1)

    return (y_p.reshape(Bp, S, D), y_s, c_p, n_p, m_p, sb_k_p, sb_v_p, sre_p, sim_p, swa_k_p, swa_v_p,
            c_s, n_s, m_s, sb_k_s, sb_v_s, sre_s, sim_s, swa_k_s, swa_v_s)
```

```python
import functools
import math

import jax
import jax.numpy as jnp
import numpy as np
from jax import lax
from jax.experimental import pallas as pl
from jax.experimental.pallas import tpu as pltpu

F32 = jnp.float32
BF16 = jnp.bfloat16

D_MODEL = 1024
PAST_LEN = 8192
ML_HEADS, ML_DK, ML_DV = 4, 64, 128
ML_QK, ML_V = ML_HEADS * ML_DK, ML_HEADS * ML_DV
SB_HEADS, SB_DH = 8, 64
SB_W = SB_HEADS * SB_DH
S5_CH, S5_GROUP, S5_GROUPS, S5_STATE = 512, 16, 32, 64
S5_N = S5_GROUPS * S5_STATE
SW_HEADS, SW_KV_HEADS, SW_DH = 8, 2, 64
SW_Q, SW_KV = SW_HEADS * SW_DH, SW_KV_HEADS * SW_DH
SW_GROUP = SW_HEADS // SW_KV_HEADS
WINDOW = 128
D_FF = 4 * D_MODEL
EPS = 1e-6

LANES = 128
SUBLANES = 8
VMEM_LIMIT = 56 * 1024 * 1024
T_PAD = SUBLANES
ML_CHUNK = 128
SB_TILE = 128
PAGES_PER_STEP = 16
S5_TT = 32


def _params(sem):
    return pltpu.CompilerParams(dimension_semantics=sem, vmem_limit_bytes=VMEM_LIMIT)


def _dot(a, b):
    return jnp.dot(a, b, preferred_element_type=F32)


def _dot_nt(a, b):
    return lax.dot_general(a, b, (((1,), (1,)), ((), ())), preferred_element_type=F32)


def _dot_tn(a, b):
    return lax.dot_general(a, b, (((0,), (0,)), ((), ())), preferred_element_type=F32)


def _rms(x, g):
    ms = jnp.mean(x * x, axis=-1, keepdims=True)
    return x * lax.rsqrt(ms + EPS) * g


def _softplus(z):
    return jnp.maximum(z, 0.0) + jnp.log1p(jnp.exp(-jnp.abs(z)))


def _sigmoid(z):
    return 1.0 / (1.0 + jnp.exp(-z))


def _split_bf16(x):
    hi = x.astype(BF16)
    lo = (x - hi.astype(F32)).astype(BF16)
    return hi, lo


def _inproj_kernel(x_ref, g_ref, w_ref, o_ref):
    o_ref[...] = _dot(_rms(x_ref[...], g_ref[...]).astype(BF16), w_ref[...])


def _inproj(x, g, w):
    M, D = x.shape
    N = w.shape[1]
    tm = min(512, M)
    return pl.pallas_call(
        _inproj_kernel, grid=(M // tm,),
        in_specs=[pl.BlockSpec((tm, D), lambda i: (i, 0)), pl.BlockSpec((1, D), lambda i: (0, 0)),
                  pl.BlockSpec((D, N), lambda i: (0, 0))],
        out_specs=pl.BlockSpec((tm, N), lambda i: (i, 0)),
        out_shape=jax.ShapeDtypeStruct((M, N), F32),
        compiler_params=_params(("parallel",)), name="inproj")(x, g, w)


EVEN_SPLIT = (2 * ML_QK + 2 * ML_V, SB_W, SB_W, SB_W)


def _inproj_even_kernel(x_ref, g_ref, w_ref, wg_ref, gb_ref, pm_ref, qs_ref, ks_ref, vs_ref, gate_ref):
    hn = _rms(x_ref[...], g_ref[...]).astype(BF16)
    res = _dot(hn, w_ref[...])
    off = 0
    for ref, width in zip((pm_ref, qs_ref, ks_ref, vs_ref), EVEN_SPLIT):
        ref[...] = res[:, off:off + width]
        off += width
    gate_ref[...] = _dot(hn, wg_ref[...]) + gb_ref[...]


def _inproj_even(x, g, w, wg, gb):
    M, D = x.shape
    N = w.shape[1]
    tm = min(512, M)
    widths = EVEN_SPLIT + (LANES,)
    return pl.pallas_call(
        _inproj_even_kernel, grid=(M // tm,),
        in_specs=[pl.BlockSpec((tm, D), lambda i: (i, 0)), pl.BlockSpec((1, D), lambda i: (0, 0)),
                  pl.BlockSpec((D, N), lambda i: (0, 0)), pl.BlockSpec((D, LANES), lambda i: (0, 0)),
                  pl.BlockSpec((1, LANES), lambda i: (0, 0))],
        out_specs=[pl.BlockSpec((tm, wd), lambda i: (i, 0)) for wd in widths],
        out_shape=[jax.ShapeDtypeStruct((M, wd), F32) for wd in widths],
        compiler_params=_params(("parallel",)), name="inproj_even")(x, g, w, wg, gb)


def _inproj_tmajor_kernel(x_ref, g_ref, w_ref, o_ref, ut_ref):
    res = _dot(_rms(x_ref[...], g_ref[...]).astype(BF16), w_ref[...])
    o_ref[...] = res
    ut_ref[...] = res[:, :S5_CH]


def _inproj_tmajor(x, g, w, *, n_seq, T):
    M, D = x.shape
    N = w.shape[1]
    tm = 512
    nt = T // tm
    return pl.pallas_call(
        _inproj_tmajor_kernel, grid=(M // tm,),
        in_specs=[pl.BlockSpec((tm, D), lambda i: (i, 0)), pl.BlockSpec((1, D), lambda i: (0, 0)),
                  pl.BlockSpec((D, N), lambda i: (0, 0))],
        out_specs=[pl.BlockSpec((tm, N), lambda i: (i, 0)),
                   pl.BlockSpec((tm, S5_CH), lambda i: (i % nt, i // nt))],
        out_shape=[jax.ShapeDtypeStruct((M, N), F32), jax.ShapeDtypeStruct((T, n_seq * S5_CH), F32)],
        compiler_params=_params(("parallel",)), name="inproj_tmajor")(x, g, w)


def _mlstm_kernel(q_ref, k_ref, v_ref, o_ref, gt_ref, gh_ref, c0_ref, n0_ref, m0_ref,
                  h_ref, c_out, n_out, m_out, caug_scr, m_scr, *, G, L, CS, last, nc):
    c = pl.program_id(1)
    first_chunk = (lambda f: f()) if nc == 1 else pl.when(c == 0)
    last_chunk = (lambda f: f()) if nc == 1 else pl.when(c == nc - 1)
    row = lax.broadcasted_iota(jnp.int32, (L, L), 0)
    col = lax.broadcasted_iota(jnp.int32, (L, L), 1)
    causal = col <= row
    eye = col == row
    e0 = (lax.broadcasted_iota(jnp.int32, (L, LANES), 1) == 0).astype(F32)
    r64 = lax.broadcasted_iota(jnp.int32, (ML_DK, ML_DK), 0)
    c64 = lax.broadcasted_iota(jnp.int32, (ML_DK, ML_DK), 1)
    eye64 = r64 == c64
    lane_dk = lax.broadcasted_iota(jnp.int32, (ML_DK, LANES), 1)
    lane_row = lax.broadcasted_iota(jnp.int32, (1, LANES), 1)
    rowid = lax.broadcasted_iota(jnp.int32, (L, 1), 0)

    def to_row(colvec):
        return jnp.sum(jnp.where(eye, colvec, 0.0), axis=0, keepdims=True)

    def seq_body(g, carry):
        @first_chunk
        def _():
            for h in range(ML_HEADS):
                nrow = n0_ref[g, pl.ds(h, 1), :]
                ncol = jnp.sum(jnp.where(eye64, nrow, 0.0), axis=1, keepdims=True)
                caug_scr[g, h] = jnp.concatenate(
                    [c0_ref[g, h], jnp.where(lane_dk == 0, ncol, 0.0)], axis=1)
                m_scr[g, h] = jnp.broadcast_to(m0_ref[g][:, h:h + 1], (SUBLANES, LANES))

        m_row = jnp.zeros((1, LANES), F32)
        for kk, h in [(kk, h) for kk in range(CS) for h in range(ML_HEADS)]:
            r0 = g * (L * CS) + kk * L
            rows = pl.ds(r0 if isinstance(r0, int) else pl.multiple_of(r0, L), L)
            gates = gt_ref[rows, :]
            q = q_ref[rows, pl.ds(h * ML_DK, ML_DK)]
            kc = k_ref[rows, pl.ds(h * ML_DK, ML_DK)] * (ML_DK ** -0.5)
            v = v_ref[rows, pl.ds(h * ML_DV, ML_DV)]
            icol = gates[:, h:h + 1]
            fcol = gates[:, ML_HEADS + h:ML_HEADS + h + 1]
            caug = caug_scr[g, h]
            m_prev = m_scr[g, h][0:1, 0:1]

            lfcol = jnp.minimum(fcol, 0.0) - jnp.log1p(jnp.exp(-jnp.abs(fcol)))
            lfrow = to_row(lfcol)
            irow = to_row(icol)
            bcol = jnp.sum(jnp.where(causal, lfrow, 0.0), axis=1, keepdims=True)
            brow = to_row(bcol)
            dmat = jnp.where(causal, bcol - brow + irow, -jnp.inf)
            inter = bcol + m_prev
            mt = jnp.maximum(inter, jnp.max(dmat, axis=1, keepdims=True))
            w_inter = jnp.exp(inter - mt)
            qb = q.astype(BF16)
            s = _dot_nt(qb, kc.astype(BF16)) * jnp.exp(dmat - mt)
            vaug = jnp.concatenate([v, e0], axis=1).astype(BF16)
            tot = w_inter * _dot(qb, caug.astype(BF16)) + _dot(s.astype(BF16), vaug)
            num = tot[:, :ML_DV]
            den = tot[:, ML_DV:ML_DV + 1]
            hh = num / jnp.maximum(jnp.abs(den), jnp.exp(-mt))

            m_new = mt[last:last + 1, :]
            g_state = jnp.exp(inter[last:last + 1, :] - m_new)
            gin = jnp.exp(bcol[last:last + 1, :] - bcol + icol - m_new)
            if last < L - 1:
                gin = jnp.where(rowid <= last, gin, 0.0)
            caug_new = g_state * caug + _dot_tn((kc * gin).astype(BF16), vaug)
            caug_scr[g, h] = caug_new
            m_scr[g, h] = jnp.broadcast_to(m_new, (SUBLANES, LANES))
            m_row = jnp.where(lane_row == h, m_new, m_row)

            hh = hh * lax.rsqrt(jnp.mean(hh * hh, axis=-1, keepdims=True) + EPS)
            hh = hh * gh_ref[:, pl.ds(h * ML_DV, ML_DV)]
            hh = hh * _sigmoid(o_ref[rows, pl.ds(h * ML_DV, ML_DV)])
            h_ref[rows, pl.ds(h * ML_DV, ML_DV)] = hh

        @last_chunk
        def _():
            for h in range(ML_HEADS):
                caug_fin = caug_scr[g, h]
                c_out[g, h] = caug_fin[:, :ML_DV]
                ncol = caug_fin[:, ML_DV:ML_DV + 1]
                n_out[g, pl.ds(h, 1), :] = jnp.sum(jnp.where(eye64, ncol, 0.0), axis=0, keepdims=True)
            m_out[g] = m_row

        return carry

    if G == 1:
        seq_body(0, 0)
    else:
        lax.fori_loop(0, G, seq_body, 0, unroll=4)


def _mlstm(pe, gates, g_head, c0, n0, m0, *, n_seq, seq_len, last):
    M = pe.shape[0]
    if seq_len >= ML_CHUNK:
        G, L = 1, ML_CHUNK
        CS = 1
    else:
        L, CS = seq_len, 1
        G = ML_CHUNK // L
    nc = seq_len // (L * CS)
    last_l = last - (seq_len // L - 1) * L
    assert CS == 1 or last_l == L - 1
    grid = (n_seq // G, nc)
    rb = lambda b, c: b * nc + c
    R = G * L * CS
    kernel = functools.partial(_mlstm_kernel, G=G, L=L, CS=CS, last=last_l, nc=nc)
    return pl.pallas_call(
        kernel, grid=grid,
        in_specs=[
            pl.BlockSpec((R, ML_QK), lambda b, c: (rb(b, c), 0)),
            pl.BlockSpec((R, ML_QK), lambda b, c: (rb(b, c), 1)),
            pl.BlockSpec((R, ML_V), lambda b, c: (rb(b, c), 1)),
            pl.BlockSpec((R, ML_V), lambda b, c: (rb(b, c), 2)),
            pl.BlockSpec((R, LANES), lambda b, c: (rb(b, c), 0)),
            pl.BlockSpec((1, ML_V), lambda b, c: (0, 0)),
            pl.BlockSpec((G, ML_HEADS, ML_DK, ML_DV), lambda b, c: (b, 0, 0, 0)),
            pl.BlockSpec((G, ML_HEADS, ML_DK), lambda b, c: (b, 0, 0)),
            pl.BlockSpec((G, 1, ML_HEADS), lambda b, c: (b, 0, 0)),
        ],
        out_specs=[
            pl.BlockSpec((R, ML_V), lambda b, c: (rb(b, c), 0)),
            pl.BlockSpec((G, ML_HEADS, ML_DK, ML_DV), lambda b, c: (b, 0, 0, 0)),
            pl.BlockSpec((G, ML_HEADS, ML_DK), lambda b, c: (b, 0, 0)),
            pl.BlockSpec((G, 1, LANES), lambda b, c: (b, 0, 0)),
        ],
        out_shape=[
            jax.ShapeDtypeStruct((M, ML_V), F32),
            jax.ShapeDtypeStruct((n_seq, ML_HEADS, ML_DK, ML_DV), F32),
            jax.ShapeDtypeStruct((n_seq, ML_HEADS, ML_DK), F32),
            jax.ShapeDtypeStruct((n_seq, 1, LANES), F32),
        ],
        scratch_shapes=[pltpu.VMEM((G, ML_HEADS, ML_DK, 2 * ML_DV), F32),
                        pltpu.VMEM((G, ML_HEADS, SUBLANES, LANES), F32)],
        compiler_params=_params(("parallel", "arbitrary")), name="mlstm",
    )(pe, pe, pe, pe, gates, g_head, c0, n0, m0.reshape(n_seq, 1, ML_HEADS))


def _suffix_matrix(n):
    r = lax.broadcasted_iota(jnp.int32, (n, n), 0)
    c = lax.broadcasted_iota(jnp.int32, (n, n), 1)
    return (r > c).astype(BF16)


def _sb_tile(z, valid, lrem, u_mat):
    sp = _softplus(z)
    lk = -sp if valid is None else jnp.where(valid, -sp, 0.0)
    hi, lo = _split_bf16(lk)
    later = _dot(hi, u_mat) + _dot(lo, u_mat)
    w = jnp.exp(z - sp + later + lrem)
    if valid is not None:
        w = jnp.where(valid, w, 0.0)
    return w, lrem + jnp.sum(lk, axis=1, keepdims=True)


def _suffix_rows(x, carry):
    K, Q = x.shape
    sub = lax.broadcasted_iota(jnp.int32, (SUBLANES, Q), 0)
    keep = {step: (sub < SUBLANES - step).astype(F32) for step in (1, 2, 4)}
    outs = [None] * (K // SUBLANES)
    for j in reversed(range(K // SUBLANES)):
        y = x[j * SUBLANES:(j + 1) * SUBLANES, :]
        for step in (1, 2, 4):
            y = y + keep[step] * pltpu.roll(y, SUBLANES - step, 0)
        outs[j] = y + carry
        carry = carry + jnp.broadcast_to(y[0:1, :], (SUBLANES, Q))
    return jnp.concatenate(outs, axis=0), carry


def _sb_prompt_kernel(bias_ref, q_ref, k_ref, v_ref, o_ref, ks_ref, vs_ref, qt_ref, acc_ref, *stage_refs, T):
    z_refs, w_refs = stage_refs[:SB_HEADS], stage_refs[SB_HEADS:]
    _sb_prompt_body(bias_ref, q_ref, k_ref, v_ref, o_ref, ks_ref, vs_ref, qt_ref, acc_ref, z_refs, w_refs, T)


def _sb_prompt_body(bias_ref, q_ref, k_ref, v_ref, o_ref, ks_ref, vs_ref, qt_ref, acc_ref, z_refs, w_refs, T):
    qi = pl.program_id(1)
    lane = lax.broadcasted_iota(jnp.int32, (1, LANES), 1)

    @pl.when(qi == 0)
    def _():
        for h in range(SB_HEADS):
            vs_ref[h] = v_ref[:, pl.ds(h * SB_DH, SB_DH)].astype(BF16)
        for hp in range(SB_HEADS // 2):
            pair = k_ref[:, pl.ds(hp * LANES, LANES)]
            ks_ref[2 * hp] = jnp.where(lane < SB_DH, pair, (lane < SB_DH + 2).astype(F32)).astype(BF16)
            ks_ref[2 * hp + 1] = jnp.where(lane >= SB_DH, pair, (lane < 2).astype(F32)).astype(BF16)

    rowi = lax.broadcasted_iota(jnp.int32, (LANES, T), 0)
    for hp in range(SB_HEADS // 2):
        qt = (q_ref[:, pl.ds(hp * LANES, LANES)] * (SB_DH ** -0.5)).T
        for h, own, r0 in ((2 * hp, rowi < SB_DH, SB_DH), (2 * hp + 1, rowi >= SB_DH, 0)):
            b = jnp.full((LANES, T), bias_ref[h], F32)
            b_hi = b.astype(BF16).astype(F32)
            aug = jnp.where(rowi == r0, b_hi, jnp.where(rowi == r0 + 1, b - b_hi, 0.0))
            qt_ref[h] = jnp.where(own, qt, aug).astype(BF16)

    row = lax.broadcasted_iota(jnp.int32, (T, T), 0)
    col = lax.broadcasted_iota(jnp.int32, (T, T), 1)
    strict = row < col

    def keys(j):
        return pl.ds(pl.multiple_of(jnp.maximum(j, 0) * T, T), T)

    def logits(h, j):
        return _dot(ks_ref[h, keys(j), :], qt_ref[h])

    def weights(zt, lrem, valid):
        sp = jnp.maximum(zt, 0.0) + jnp.log(1.0 + jnp.exp(-jnp.abs(zt)))
        if valid is not None:
            sp = jnp.where(valid, sp, 0.0)
        y, lrem = _suffix_rows(sp, lrem)
        w = jnp.exp(zt - y)
        if valid is not None:
            w = jnp.where(valid, w, 0.0)
        return w.astype(BF16), lrem

    def values(h, j, w):
        return _dot_tn(w, vs_ref[h, keys(j), :])

    lrems = []
    for h in range(SB_HEADS):
        w, lrem = weights(logits(h, qi), jnp.zeros((SUBLANES, T), F32), strict)
        acc_ref[h] = jnp.zeros((T, SB_DH), F32)
        lrems.append(lrem)
        w_refs[h][1] = w
        z_refs[h][0] = logits(h, qi - 1)

    def body(it, lrems):
        j = qi - 1 - it
        cur = lax.rem(it, 2)
        nxt = 1 - cur
        out = []
        for h in range(SB_HEADS):
            acc_ref[h] += values(h, j + 1, w_refs[h][nxt])
            w, lrem = weights(z_refs[h][cur], lrems[h], None)
            w_refs[h][cur] = w
            z_refs[h][nxt] = logits(h, j - 1)
            out.append(lrem)
        return tuple(out)

    lax.fori_loop(0, qi, body, tuple(lrems))
    last = lax.rem(qi + 1, 2)
    o_ref[...] = jnp.concatenate(
        [acc_ref[h] + values(h, 0, w_refs[h][last]) for h in range(SB_HEADS)], axis=1)


def _sb_prompt(q, k, v, sb_bias, *, n_seq, seq_len):
    M = q.shape[0]
    T = SB_TILE
    nq = seq_len // T
    kernel = functools.partial(_sb_prompt_kernel, T=T)
    return pl.pallas_call(
        kernel, grid=(n_seq, nq),
        in_specs=[
            pl.BlockSpec(memory_space=pltpu.SMEM),
            pl.BlockSpec((T, SB_W), lambda b, i: (b * nq + i, 0)),
            pl.BlockSpec((seq_len, SB_W), lambda b, i: (b, 0)),
            pl.BlockSpec((seq_len, SB_W), lambda b, i: (b, 0)),
        ],
        out_specs=pl.BlockSpec((T, SB_W), lambda b, i: (b * nq + i, 0)),
        out_shape=jax.ShapeDtypeStruct((M, SB_W), F32),
        scratch_shapes=[pltpu.VMEM((SB_HEADS, seq_len, LANES), BF16),
                        pltpu.VMEM((SB_HEADS, seq_len, SB_DH), BF16),
                        pltpu.VMEM((SB_HEADS, LANES, T), BF16),
                        pltpu.VMEM((SB_HEADS, T, SB_DH), F32)]
        + [pltpu.VMEM((2, T, T), F32)] * SB_HEADS + [pltpu.VMEM((2, T, T), BF16)] * SB_HEADS,
        compiler_params=_params(("parallel", "arbitrary")), name="sb_prompt",
    )(sb_bias, q, k, v)


def _sb_sample_kernel(pt_ref, bias_ref, qbd_ref, kn_ref, vn_ref, ck_hbm, cv_hbm, o_ref,
                      kbuf, vbuf, sem, acc_ref, lrem_ref, *, P, T, n_pages):
    b, s = pl.program_id(0), pl.program_id(1)
    n_steps = pl.num_programs(1)
    g = b * n_steps + s
    slot = lax.rem(g, 2)

    def page_copies(bb, ss, sl, lookup):
        out = []
        for i in range(P):
            page = pt_ref[bb, n_pages - 1 - (ss * P + i)] if lookup else 0
            out.append(pltpu.make_async_copy(ck_hbm.at[page], kbuf.at[sl, i], sem.at[sl, 0]))
            out.append(pltpu.make_async_copy(cv_hbm.at[page], vbuf.at[sl, i], sem.at[sl, 1]))
        return out

    @pl.when(g == 0)
    def _():
        for c in page_copies(0, 0, 0, True):
            c.start()

    @pl.when(g + 1 < pl.num_programs(0) * n_steps)
    def _():
        wrap = s + 1 == n_steps
        for c in page_copies(jnp.where(wrap, b + 1, b), jnp.where(wrap, 0, s + 1), 1 - slot, True):
            c.start()

    for c in page_copies(b, s, slot, False):
        c.wait()

    R = SB_HEADS * T
    u_mat = _suffix_matrix(LANES)
    qb = (qbd_ref[0] * (SB_DH ** -0.5)).astype(BF16)
    bias = bias_ref[...]

    @pl.when(s == 0)
    def _():
        pad = jnp.zeros((LANES - T, SB_W), F32)
        kn = jnp.concatenate([kn_ref[0], pad], axis=0).astype(BF16)
        vn = jnp.concatenate([vn_ref[0], pad], axis=0).astype(BF16)
        t_of_row = lax.broadcasted_iota(jnp.int32, (R, LANES), 0) % T
        colk = lax.broadcasted_iota(jnp.int32, (R, LANES), 1)
        z = _dot_nt(qb, kn) + bias
        w, lrem = _sb_tile(z, colk < t_of_row, jnp.zeros((R, 1), F32), u_mat)
        acc_ref[...] = _dot(w.astype(BF16), vn)
        lrem_ref[...] = jnp.broadcast_to(lrem, (R, LANES))

    acc = acc_ref[...]
    lrem = lrem_ref[:, 0:1]
    kt = jnp.concatenate([kbuf[slot, i].reshape(SB_W, LANES) for i in range(P)], axis=1).astype(BF16)
    z_all = _dot(qb, kt)
    z = jnp.concatenate([z_all[:, i * LANES:(i + 1) * LANES] for i in range(P)], axis=0)
    z = z + jnp.concatenate([bias] * P, axis=0)
    sp = _softplus(z)
    hi, lo = _split_bf16(-sp)
    later = _dot(hi, u_mat) + _dot(lo, u_mat)
    tot = jnp.sum(sp, axis=1, keepdims=True)
    lrems = []
    for i in range(P):
        lrems.append(lrem)
        lrem = lrem - tot[i * R:(i + 1) * R]
    w = jnp.exp(z - sp + later + jnp.concatenate(lrems, axis=0)).astype(BF16)
    w_all = jnp.concatenate([w[i * R:(i + 1) * R] for i in range(P)], axis=1)
    v = jnp.concatenate([vbuf[slot, i].reshape(SB_W, LANES).T for i in range(P)], axis=0).astype(BF16)
    acc = acc + _dot(w_all, v)
    acc_ref[...] = acc
    lrem_ref[...] = jnp.broadcast_to(lrem, (R, LANES))

    @pl.when(s == pl.num_programs(1) - 1)
    def _():
        o_ref[0] = jnp.concatenate(
            [acc[h * T:(h + 1) * T, h * SB_DH:(h + 1) * SB_DH] for h in range(SB_HEADS)], axis=1)


def _sb_sample(qbd, bias_col, k_new, v_new, cache_k, cache_v, page_table):
    DB, R, _ = qbd.shape
    T = R // SB_HEADS
    n_pages = page_table.shape[1]
    P = PAGES_PER_STEP
    page = cache_k.shape[1]
    assert page == LANES
    ck = cache_k.transpose(0, 2, 3, 1)
    cv = cache_v.transpose(0, 2, 3, 1)

    grid_spec = pltpu.PrefetchScalarGridSpec(
        num_scalar_prefetch=1, grid=(DB, n_pages // P),
        in_specs=[pl.BlockSpec((R, 1), lambda b, s, pt: (0, 0)),
                  pl.BlockSpec((1, R, SB_W), lambda b, s, pt: (b, 0, 0)),
                  pl.BlockSpec((1, T, SB_W), lambda b, s, pt: (b, 0, 0)),
                  pl.BlockSpec((1, T, SB_W), lambda b, s, pt: (b, 0, 0)),
                  pl.BlockSpec(memory_space=pl.ANY), pl.BlockSpec(memory_space=pl.ANY)],
        out_specs=pl.BlockSpec((1, T, SB_W), lambda b, s, pt: (b, 0, 0)),
        scratch_shapes=[pltpu.VMEM((2, P, SB_HEADS, SB_DH, page), F32),
                        pltpu.VMEM((2, P, SB_HEADS, SB_DH, page), F32),
                        pltpu.SemaphoreType.DMA((2, 2)),
                        pltpu.VMEM((R, SB_W), F32), pltpu.VMEM((R, LANES), F32)])
    kernel = functools.partial(_sb_sample_kernel, P=P, T=T, n_pages=n_pages)
    return pl.pallas_call(
        kernel, grid_spec=grid_spec, out_shape=jax.ShapeDtypeStruct((DB, T, SB_W), F32),
        compiler_params=_params(("arbitrary", "arbitrary")), name="sb_sample",
    )(page_table, bias_col, qbd, k_new, v_new, ck, cv)


def _s5_param_kernel(lre_ref, lim_ref, ldt_ref, bre_ref, bim_ref, are_ref, aim_ref, bbre_ref, bbim_ref):
    lre = jnp.minimum(lre_ref[...], -1e-4)
    lim = lim_ref[...]
    dt = jnp.exp(ldt_ref[...])
    mag = jnp.exp(lre * dt)
    ab_re = mag * jnp.cos(lim * dt)
    ab_im = mag * jnp.sin(lim * dt)
    den = lre * lre + lim * lim
    c_re = ((ab_re - 1.0) * lre + ab_im * lim) / den
    c_im = (ab_im * lre - (ab_re - 1.0) * lim) / den
    b_re, b_im = bre_ref[...], bim_ref[...]
    are_ref[...] = ab_re
    aim_ref[...] = ab_im
    bbre_ref[...] = c_re * b_re - c_im * b_im
    bbim_ref[...] = c_re * b_im + c_im * b_re


def _s5_params(lam_re, lam_im, log_dt, b_re, b_im):
    N = S5_N
    col = lambda a: a.reshape(N, 1)
    ldt = jnp.repeat(log_dt, S5_STATE).reshape(N, 1)
    full = lambda shape: pl.BlockSpec(shape, lambda: (0,) * len(shape))
    are, aim, bbre, bbim = pl.pallas_call(
        _s5_param_kernel,
        in_specs=[full((N, 1))] * 3 + [full((N, S5_GROUP))] * 2,
        out_specs=[full((N, 1))] * 2 + [full((N, S5_GROUP))] * 2,
        out_shape=[jax.ShapeDtypeStruct((N, 1), F32)] * 2 + [jax.ShapeDtypeStruct((N, S5_GROUP), F32)] * 2,
        name="s5_params",
    )(col(lam_re), col(lam_im), ldt, b_re.reshape(N, S5_GROUP), b_im.reshape(N, S5_GROUP))
    return are.reshape(1, N), aim.reshape(1, N), bbre, bbim


def _block_diag(blocks):
    G, r, c = blocks.shape
    eye = jnp.eye(G, dtype=blocks.dtype)
    return (blocks[:, :, None, :] * eye[:, None, :, None]).reshape(G * r, G * c)


def _s5_kernel(u_ref, are_ref, aim_ref, bb_ref, cre_ref, cim_ref, d_ref, wg_ref, bg_ref, sre0_ref, sim0_ref,
               y_ref, sre_out, sim_out, bu_scr, xs_scr, st_scr, *, TT, t_last):
    tb = pl.program_id(1)
    N = S5_N
    CH = 512
    BS = SUBLANES

    @pl.when(tb == 0)
    def _():
        st_scr[:, 0:N] = sre0_ref[...]
        st_scr[:, N:2 * N] = sim0_ref[...]

    u = u_ref[...].reshape(TT * BS, S5_CH)
    ub = u.astype(BF16)
    UH, NH = S5_CH // 2, N // 2
    for hf in range(2):
        res = _dot(ub[:, hf * UH:(hf + 1) * UH], bb_ref[hf])
        bu_scr[:, hf * NH:(hf + 1) * NH] = res[:, :NH]
        bu_scr[:, N + hf * NH:N + (hf + 1) * NH] = res[:, NH:]

    for cidx in range(N // CH):
        lo = cidx * CH
        ar = jnp.broadcast_to(are_ref[:, lo:lo + CH], (BS, CH))
        ai = jnp.broadcast_to(aim_ref[:, lo:lo + CH], (BS, CH))

        def step(t, carry):
            xr, xi = carry
            r = pl.ds(pl.multiple_of(t * BS, BS), BS)
            br = bu_scr[r, lo:lo + CH]
            bi = bu_scr[r, N + lo:N + lo + CH]
            nr = ar * xr - ai * xi + br
            ni = ar * xi + ai * xr + bi
            xs_scr[r, lo:lo + CH] = nr
            xs_scr[r, N + lo:N + lo + CH] = ni
            return nr, ni

        xr, xi = lax.fori_loop(0, TT, step, (st_scr[:, lo:lo + CH], st_scr[:, N + lo:N + lo + CH]))
        st_scr[:, lo:lo + CH] = xr
        st_scr[:, N + lo:N + lo + CH] = xi

    @pl.when(tb == t_last // TT)
    def _():
        r = pl.ds((t_last % TT) * BS, BS)
        sre_out[...] = xs_scr[r, 0:N]
        sim_out[...] = xs_scr[r, N:2 * N]

    ys = []
    for hf in range(2):
        x_re = xs_scr[:, hf * NH:(hf + 1) * NH].astype(BF16)
        x_im = xs_scr[:, N + hf * NH:N + (hf + 1) * NH].astype(BF16)
        ys.append(_dot(x_re, cre_ref[hf]) - _dot(x_im, cim_ref[hf]))
    y = jnp.concatenate(ys, axis=1) + d_ref[...] * u
    y = jax.nn.gelu(y)
    y = y * _sigmoid(_dot(y.astype(BF16), wg_ref[...]) + bg_ref[...])
    y_ref[...] = y.reshape(TT, BS, S5_CH)


def _s5(u_t, are, aim, bb, cre, cim, d, wg, bg, sre0, sim0, *, t_last):
    T, B, _ = u_t.shape
    TT = min(S5_TT, T)
    N = S5_N
    const = lambda shape: pl.BlockSpec(shape, lambda b, t: (0,) * len(shape))
    kernel = functools.partial(_s5_kernel, TT=TT, t_last=t_last)
    return pl.pallas_call(
        kernel, grid=(B // SUBLANES, T // TT),
        in_specs=[pl.BlockSpec((TT, SUBLANES, S5_CH), lambda b, t: (t, b, 0)),
                  const((1, N)), const((1, N)), const((2, S5_CH // 2, N)), const((2, N // 2, S5_CH // 2)),
                  const((2, N // 2, S5_CH // 2)),
                  const((1, S5_CH)), const((S5_CH, S5_CH)), const((1, S5_CH)),
                  pl.BlockSpec((SUBLANES, N), lambda b, t: (b, 0)),
                  pl.BlockSpec((SUBLANES, N), lambda b, t: (b, 0))],
        out_specs=[pl.BlockSpec((TT, SUBLANES, S5_CH), lambda b, t: (t, b, 0)),
                   pl.BlockSpec((SUBLANES, N), lambda b, t: (b, 0)),
                   pl.BlockSpec((SUBLANES, N), lambda b, t: (b, 0))],
        out_shape=[jax.ShapeDtypeStruct((T, B, S5_CH), F32),
                   jax.ShapeDtypeStruct((B, N), F32), jax.ShapeDtypeStruct((B, N), F32)],
        scratch_shapes=[pltpu.VMEM((TT * SUBLANES, 2 * N), F32), pltpu.VMEM((TT * SUBLANES, 2 * N), F32),
                        pltpu.VMEM((SUBLANES, 2 * N), F32)],
        compiler_params=_params(("parallel", "arbitrary")), name="s5",
    )(u_t, are, aim, bb, cre, cim, d, wg, bg, sre0, sim0)


def _alibi_slope(h):
    return float(2.0 ** (-8.0 * (h + 1) / SW_HEADS))


def _sink_softmax_parts(zs, sink):
    m = sink
    for z in zs:
        m = jnp.maximum(m, jnp.max(z, axis=-1, keepdims=True))
    es = [jnp.exp(z - m) for z in zs]
    tot = jnp.exp(sink - m)
    for e in es:
        tot = tot + jnp.sum(e, axis=-1, keepdims=True)
    return [e / tot for e in es]


def _swa_prompt_kernel(sink_ref, q_ref, kp_ref, kc_ref, vp_ref, vc_ref, o_ref):
    i = pl.program_id(1)
    W = WINDOW
    rowq = lax.broadcasted_iota(jnp.int32, (W, 2 * W), 0)
    colc = lax.broadcasted_iota(jnp.int32, (W, 2 * W), 1)
    dist = W + rowq - colc
    valid = (dist >= 0) & (dist < W) & (colc >= jnp.where(i > 0, 0, W))
    distf = dist.astype(F32)
    outs = []
    for kv in range(SW_KV_HEADS):
        ls = pl.ds(kv * SW_DH, SW_DH)
        kband = jnp.concatenate([kp_ref[:, ls], kc_ref[:, ls]], axis=0).astype(BF16)
        vband = jnp.concatenate([vp_ref[:, ls], vc_ref[:, ls]], axis=0).astype(BF16)
        for gq in range(SW_GROUP):
            h = kv * SW_GROUP + gq
            qh = q_ref[:, pl.ds(h * SW_DH, SW_DH)].astype(BF16)
            z = _dot_nt(qh, kband) * (SW_DH ** -0.5) - _alibi_slope(h) * distf
            z = jnp.where(valid, z, -jnp.inf)
            (p,) = _sink_softmax_parts([z], sink_ref[h])
            outs.append(_dot(p.astype(BF16), vband))
    o_ref[...] = jnp.concatenate(outs, axis=1)


def _swa_prompt(po, sinks, *, n_seq, seq_len):
    M = po.shape[0]
    W = WINDOW
    nb = seq_len // W
    kcol, vcol = (S5_CH + SW_Q) // SW_KV, (S5_CH + SW_Q) // SW_KV + 1
    cur = lambda b, i: b * nb + i
    prev = lambda b, i: b * nb + jnp.maximum(i - 1, 0)
    return pl.pallas_call(
        _swa_prompt_kernel, grid=(n_seq, nb),
        in_specs=[pl.BlockSpec(memory_space=pltpu.SMEM),
                  pl.BlockSpec((W, SW_Q), lambda b, i: (cur(b, i), 1)),
                  pl.BlockSpec((W, SW_KV), lambda b, i: (prev(b, i), kcol)),
                  pl.BlockSpec((W, SW_KV), lambda b, i: (cur(b, i), kcol)),
                  pl.BlockSpec((W, SW_KV), lambda b, i: (prev(b, i), vcol)),
                  pl.BlockSpec((W, SW_KV), lambda b, i: (cur(b, i), vcol))],
        out_specs=pl.BlockSpec((W, SW_Q), lambda b, i: (cur(b, i), 0)),
        out_shape=jax.ShapeDtypeStruct((M, SW_Q), F32),
        compiler_params=_params(("parallel", "arbitrary")), name="swa_prompt",
    )(sinks, po, po, po, po, po)


def _swa_sample_kernel(sink_ref, q_ref, kn_ref, vn_ref, ck_ref, cv_ref, o_ref, *, G, T):
    W = WINDOW
    R = SW_GROUP * T
    t_c = lax.broadcasted_iota(jnp.int32, (R, W), 0) % T
    j_c = lax.broadcasted_iota(jnp.int32, (R, W), 1)
    dist_c = W + t_c - j_c
    valid_c = dist_c < W
    t_n = lax.broadcasted_iota(jnp.int32, (R, T), 0) % T
    s_n = lax.broadcasted_iota(jnp.int32, (R, T), 1)
    dist_n = t_n - s_n
    valid_n = dist_n >= 0
    g_of_row = lax.broadcasted_iota(jnp.int32, (R, 1), 0) // T

    def seq_body(g, carry):
        rows = pl.ds(pl.multiple_of(g * T, T), T)
        outs = []
        for kv in range(SW_KV_HEADS):
            ls = pl.ds(kv * SW_DH, SW_DH)
            slope = jnp.zeros((R, 1), F32)
            sink = jnp.zeros((R, 1), F32)
            for gq in range(SW_GROUP):
                h = kv * SW_GROUP + gq
                slope = jnp.where(g_of_row == gq, _alibi_slope(h), slope)
                sink = jnp.where(g_of_row == gq, sink_ref[h], sink)
            q4 = jnp.concatenate(
                [q_ref[rows, pl.ds((kv * SW_GROUP + gq) * SW_DH, SW_DH)] for gq in range(SW_GROUP)],
                axis=0).astype(BF16)
            kc = ck_ref[g, :, ls].astype(BF16)
            vc = cv_ref[g, :, ls].astype(BF16)
            kn = kn_ref[rows, ls].astype(BF16)
            vn = vn_ref[rows, ls].astype(BF16)
            zc = _dot_nt(q4, kc) * (SW_DH ** -0.5) - slope * dist_c.astype(F32)
            zn = _dot_nt(q4, kn) * (SW_DH ** -0.5) - slope * dist_n.astype(F32)
            zc = jnp.where(valid_c, zc, -jnp.inf)
            zn = jnp.where(valid_n, zn, -jnp.inf)
            pc, pn = _sink_softmax_parts([zc, zn], sink)
            o4 = _dot(pc.astype(BF16), vc) + _dot(pn.astype(BF16), vn)
            outs += [o4[gq * T:(gq + 1) * T] for gq in range(SW_GROUP)]
        o_ref[rows, :] = jnp.concatenate(outs, axis=1)
        return carry

    lax.fori_loop(0, G, seq_body, 0, unroll=4)


def _swa_sample(po, sinks, cache_k, cache_v, *, n_seq, T):
    M = po.shape[0]
    G = 16
    kcol, vcol = (S5_CH + SW_Q) // SW_KV, (S5_CH + SW_Q) // SW_KV + 1
    kernel = functools.partial(_swa_sample_kernel, G=G, T=T)
    return pl.pallas_call(
        kernel, grid=(n_seq // G,),
        in_specs=[pl.BlockSpec(memory_space=pltpu.SMEM),
                  pl.BlockSpec((G * T, SW_Q), lambda b: (b, 1)),
                  pl.BlockSpec((G * T, SW_KV), lambda b: (b, kcol)),
                  pl.BlockSpec((G * T, SW_KV), lambda b: (b, vcol)),
                  pl.BlockSpec((G, WINDOW, SW_KV), lambda b: (b, 0, 0)),
                  pl.BlockSpec((G, WINDOW, SW_KV), lambda b: (b, 0, 0))],
        out_specs=pl.BlockSpec((G * T, SW_Q), lambda b: (b, 0)),
        out_shape=jax.ShapeDtypeStruct((M, SW_Q), F32),
        compiler_params=_params(("parallel",)), name="swa_sample",
    )(sinks, po, po, po, cache_k, cache_v)


def _post_kernel(x_ref, a_ref, b_ref, wa_ref, wb_ref, g_ref, w1_ref, w2_ref, gf_ref, o_ref,
                 x1_scr, hn_scr, acc_scr, *, final):
    j = pl.program_id(1)

    @pl.when(j == 0)
    def _():
        x1 = x_ref[...] + _dot(a_ref[...].astype(BF16), wa_ref[...]) + _dot(b_ref[...].astype(BF16), wb_ref[...])
        x1_scr[...] = x1
        hn_scr[...] = _rms(x1, g_ref[...]).astype(BF16)
        acc_scr[...] = jnp.zeros_like(acc_scr)

    hmid = jnp.square(jnp.maximum(_dot(hn_scr[...], w1_ref[...]), 0.0))
    acc_scr[...] += _dot(hmid.astype(BF16), w2_ref[...])

    @pl.when(j == pl.num_programs(1) - 1)
    def _():
        y = x1_scr[...] + acc_scr[...]
        if final:
            y = _rms(y, gf_ref[...])
        o_ref[...] = y


def _post(x, a, b, wa, wb, g, w1, w2, gf, *, final, a_tmajor_seq_len=None):
    M, D = x.shape
    tm = min(512, M)
    tf = 2048
    Ka, Kb = wa.shape[0], wb.shape[0]
    if a_tmajor_seq_len is None:
        a_map = lambda i, j: (i, 0)
    else:
        nt = a_tmajor_seq_len // tm
        a_map = lambda i, j: (i % nt, i // nt)
    kernel = functools.partial(_post_kernel, final=final)
    return pl.pallas_call(
        kernel, grid=(M // tm, D_FF // tf),
        in_specs=[pl.BlockSpec((tm, D), lambda i, j: (i, 0)),
                  pl.BlockSpec((tm, Ka), a_map),
                  pl.BlockSpec((tm, Kb), lambda i, j: (i, 0)),
                  pl.BlockSpec((Ka, D), lambda i, j: (0, 0)),
                  pl.BlockSpec((Kb, D), lambda i, j: (0, 0)),
                  pl.BlockSpec((1, D), lambda i, j: (0, 0)),
                  pl.BlockSpec((D, tf), lambda i, j: (0, j)),
                  pl.BlockSpec((tf, D), lambda i, j: (j, 0)),
                  pl.BlockSpec((1, D), lambda i, j: (0, 0))],
        out_specs=pl.BlockSpec((tm, D), lambda i, j: (i, 0)),
        out_shape=jax.ShapeDtypeStruct((M, D), F32),
        scratch_shapes=[pltpu.VMEM((tm, D), F32), pltpu.VMEM((tm, D), BF16), pltpu.VMEM((tm, D), F32)],
        compiler_params=_params(("parallel", "arbitrary")), name="post",
    )(x, a, b, wa, wb, g, w1, w2, gf)


def _trunk(x, ml_state, s5_state, sb_fn, swa_fn, p, *, n_seq, T, t_real):
    last = t_real - 1
    pm, q_sb, k_sb, v_sb, gates = _inproj_even(x, p["g_mix0"], p["w_even"], p["w_gate"], p["b_gate"])
    h_ml, c_new, n_new, m_new = _mlstm(pm, gates, p["g_head"], *ml_state, n_seq=n_seq, seq_len=T, last=last)
    h_sb = sb_fn(q_sb, k_sb, v_sb)
    x = _post(x, h_ml, h_sb, p["wo_even_a"], p["wo_even_b"], p["g_ffn0"], p["w1_0"], p["w2_0"],
              p["g_final"], final=False)
    tmajor = T % 512 == 0
    if tmajor:
        po, u_t = _inproj_tmajor(x, p["g_mix1"], p["w_odd"], n_seq=n_seq, T=T)
        u_t = u_t.reshape(T, n_seq, S5_CH)
    else:
        po = _inproj(x, p["g_mix1"], p["w_odd"])
        u_t = po[:, :S5_CH].reshape(n_seq, T, S5_CH).transpose(1, 0, 2)
    y_t, s_re, s_im = _s5(u_t, p["a_re"], p["a_im"], p["bb"], p["cc_re"], p["cc_im"], p["s5_d"], p["w_glu"],
                          p["b_glu"], s5_state[0].reshape(n_seq, S5_N), s5_state[1].reshape(n_seq, S5_N),
                          t_last=last)
    o_sw = swa_fn(po)
    if tmajor:
        y_s5, seq_len = y_t.reshape(T, n_seq * S5_CH), T
    else:
        y_s5, seq_len = y_t.transpose(1, 0, 2).reshape(n_seq * T, S5_CH), None
    x = _post(x, y_s5, o_sw, p["wo_odd_a"], p["wo_odd_b"], p["g_ffn1"], p["w1_1"], p["w2_1"],
              p["g_final"], final=True, a_tmajor_seq_len=seq_len)
    state = (c_new, n_new, m_new[:, 0, :ML_HEADS], k_sb, v_sb, s_re.reshape(n_seq, S5_GROUPS, S5_STATE),
             s_im.reshape(n_seq, S5_GROUPS, S5_STATE), po)
    return x, state


def kernel(x_prompt, x_sample, state_mlstm_C, state_mlstm_n, state_mlstm_m, cache_sb_k, cache_sb_v, page_table, state_s5_re, state_s5_im, cache_swa_k, cache_swa_v, g_norm_mix, g_norm_ffn, g_norm_final, w_in_even, b_igate, b_fgate, g_mlstm_head, sb_bias, w_out_even, w_in_odd, s5_lambda_re, s5_lambda_im, s5_log_dt, s5_B_re, s5_B_im, s5_C_re, s5_C_im, s5_D, w_glu, b_glu, swa_sinks, w_out_odd, w_ff1, w_ff2):
    Bp, S, D = x_prompt.shape
    DB, Ts, _ = x_sample.shape
    n_gate = 2 * ML_HEADS
    g0 = 2 * ML_QK + 2 * ML_V
    row = lambda a: a.reshape(1, -1).astype(F32)

    a_re, a_im, bb_re, bb_im = _s5_params(s5_lambda_re, s5_lambda_im, s5_log_dt, s5_B_re, s5_B_im)
    bbt = lambda bb: _block_diag(bb.reshape(S5_GROUPS, S5_STATE, S5_GROUP).transpose(0, 2, 1))
    half_blocks = lambda m, hf: m[hf * m.shape[0] // 2:(hf + 1) * m.shape[0] // 2,
                                  hf * m.shape[1] // 2:(hf + 1) * m.shape[1] // 2]
    p = {
        "g_mix0": row(g_norm_mix[0]), "g_mix1": row(g_norm_mix[1]),
        "g_ffn0": row(g_norm_ffn[0]), "g_ffn1": row(g_norm_ffn[1]), "g_final": row(g_norm_final),
        "w_even": jnp.concatenate([w_in_even[:, :g0], w_in_even[:, g0 + n_gate:]], axis=1).astype(BF16),
        "w_gate": jnp.pad(w_in_even[:, g0:g0 + n_gate], ((0, 0), (0, LANES - n_gate))).astype(BF16),
        "b_gate": jnp.pad(jnp.concatenate([b_igate, b_fgate]), (0, LANES - n_gate)).reshape(1, LANES).astype(F32),
        "g_head": row(g_mlstm_head),
        "wo_even_a": w_out_even[:ML_V].astype(BF16), "wo_even_b": w_out_even[ML_V:].astype(BF16),
        "w_odd": w_in_odd.astype(BF16),
        "a_re": a_re, "a_im": a_im,
        "bb": jnp.stack([jnp.concatenate([half_blocks(bbt(bb_re), hf), half_blocks(bbt(bb_im), hf)], axis=1)
                         for hf in range(2)]).astype(BF16),
        "cc_re": jnp.stack([half_blocks(_block_diag(s5_C_re.transpose(0, 2, 1)), hf)
                            for hf in range(2)]).astype(BF16),
        "cc_im": jnp.stack([half_blocks(_block_diag(s5_C_im.transpose(0, 2, 1)), hf)
                            for hf in range(2)]).astype(BF16),
        "s5_d": row(s5_D), "w_glu": w_glu.astype(BF16), "b_glu": row(b_glu),
        "wo_odd_a": w_out_odd[:S5_CH].astype(BF16), "wo_odd_b": w_out_odd[S5_CH:].astype(BF16),
        "w1_0": w_ff1[0].astype(BF16), "w2_0": w_ff2[0].astype(BF16),
        "w1_1": w_ff1[1].astype(BF16), "w2_1": w_ff2[1].astype(BF16),
    }
    sb_bias = sb_bias.astype(F32)
    sinks = swa_sinks.astype(F32)

    ml0 = (jnp.zeros((Bp, ML_HEADS, ML_DK, ML_DV), F32), jnp.zeros((Bp, ML_HEADS, ML_DK), F32),
           jnp.zeros((Bp, ML_HEADS), F32))
    s50 = (jnp.zeros((Bp, S5_GROUPS, S5_STATE), F32), jnp.zeros((Bp, S5_GROUPS, S5_STATE), F32))
    y_p, (c_p, n_p, m_p, ksb_p, vsb_p, sre_p, sim_p, po_p) = _trunk(
        x_prompt.reshape(Bp * S, D), ml0, s50,
        functools.partial(_sb_prompt, sb_bias=sb_bias, n_seq=Bp, seq_len=S),
        functools.partial(_swa_prompt, sinks=sinks, n_seq=Bp, seq_len=S),
        p, n_seq=Bp, T=S, t_real=S)
    sb_k_p = ksb_p.reshape(Bp, S, SB_HEADS, SB_DH)
    sb_v_p = vsb_p.reshape(Bp, S, SB_HEADS, SB_DH)
    po3 = po_p.reshape(Bp, S, -1)
    swa_k_p = po3[:, -WINDOW:, S5_CH + SW_Q:S5_CH + SW_Q + SW_KV].reshape(Bp, WINDOW, SW_KV_HEADS, SW_DH)
    swa_v_p = po3[:, -WINDOW:, S5_CH + SW_Q + SW_KV:].reshape(Bp, WINDOW, SW_KV_HEADS, SW_DH)

    T = T_PAD
    xs = jnp.pad(x_sample, ((0, 0), (0, T - Ts), (0, 0))).reshape(DB * T, D)
    bias_col = jnp.repeat(sb_bias, T).reshape(SB_HEADS * T, 1)
    head_eye = jnp.eye(SB_HEADS, dtype=F32)

    def sb_sample_fn(q_sb, k_sb, v_sb):
        q = q_sb.reshape(DB, T, SB_HEADS, SB_DH)
        qbd = (q.transpose(0, 2, 1, 3)[:, :, :, None, :] * head_eye[None, :, None, :, None]).reshape(
            DB, SB_HEADS * T, SB_W)
        out = _sb_sample(qbd, bias_col, k_sb.reshape(DB, T, SB_W), v_sb.reshape(DB, T, SB_W),
                         cache_sb_k, cache_sb_v, page_table)
        return out.reshape(DB * T, SB_W)

    ck = cache_swa_k.reshape(DB, WINDOW, SW_KV)
    cv = cache_swa_v.reshape(DB, WINDOW, SW_KV)
    y_s, (c_s, n_s, m_s, ksb_s, vsb_s, sre_s, sim_s, po_s) = _trunk(
        xs, (state_mlstm_C, state_mlstm_n, state_mlstm_m), (state_s5_re, state_s5_im),
        sb_sample_fn, functools.partial(_swa_sample, sinks=sinks, cache_k=ck, cache_v=cv, n_seq=DB, T=T),
        p, n_seq=DB, T=T, t_real=Ts)
    y_s = y_s.reshape(DB, T, D)[:, :Ts]
    sb_k_s = ksb_s.reshape(DB, T, SB_HEADS, SB_DH)[:, :Ts]
    sb_v_s = vsb_s.reshape(DB, T, SB_HEADS, SB_DH)[:, :Ts]
    po_s3 = po_s.reshape(DB, T, -1)[:, :Ts]
    k_new = po_s3[:, :, S5_CH + SW_Q:S5_CH + SW_Q + SW_KV].reshape(DB, Ts, SW_KV_HEADS, SW_DH)
    v_new = po_s3[:, :, S5_CH + SW_Q + SW_KV:].reshape(DB, Ts, SW_KV_HEADS, SW_DH)
    swa_k_s = jnp.concatenate([cache_swa_k[:, Ts:], k_new], axis=1)
    swa_v_s = jnp.concatenate([cache_swa_v[:, Ts:], v_new], axis=1)

    return (y_p.reshape(Bp, S, D), y_s, c_p, n_p, m_p, sb_k_p, sb_v_p, sre_p, sim_p, swa_k_p, swa_v_p,
            c_s, n_s, m_s, sb_k_s, sb_v_s, sre_s, sim_s, swa_k_s, swa_v_s)
```

```python
import functools
import math

import jax
import jax.numpy as jnp
import numpy as np
from jax import lax
from jax.experimental import pallas as pl
from jax.experimental.pallas import tpu as pltpu

F32 = jnp.float32
BF16 = jnp.bfloat16

D_MODEL = 1024
PAST_LEN = 8192
ML_HEADS, ML_DK, ML_DV = 4, 64, 128
ML_QK, ML_V = ML_HEADS * ML_DK, ML_HEADS * ML_DV
SB_HEADS, SB_DH = 8, 64
SB_W = SB_HEADS * SB_DH
S5_CH, S5_GROUP, S5_GROUPS, S5_STATE = 512, 16, 32, 64
S5_N = S5_GROUPS * S5_STATE
SW_HEADS, SW_KV_HEADS, SW_DH = 8, 2, 64
SW_Q, SW_KV = SW_HEADS * SW_DH, SW_KV_HEADS * SW_DH
SW_GROUP = SW_HEADS // SW_KV_HEADS
WINDOW = 128
D_FF = 4 * D_MODEL
EPS = 1e-6

LANES = 128
SUBLANES = 8
VMEM_LIMIT = 56 * 1024 * 1024
T_PAD = SUBLANES
ML_CHUNK = 128
SB_TILE = 128
PAGES_PER_STEP = 32
S5_TT = 32


def _params(sem):
    return pltpu.CompilerParams(dimension_semantics=sem, vmem_limit_bytes=VMEM_LIMIT)


def _dot(a, b):
    return jnp.dot(a, b, preferred_element_type=F32)


def _dot_nt(a, b):
    return lax.dot_general(a, b, (((1,), (1,)), ((), ())), preferred_element_type=F32)


def _dot_tn(a, b):
    return lax.dot_general(a, b, (((0,), (0,)), ((), ())), preferred_element_type=F32)


def _rms(x, g):
    ms = jnp.mean(x * x, axis=-1, keepdims=True)
    return x * lax.rsqrt(ms + EPS) * g


def _softplus(z):
    return jnp.maximum(z, 0.0) + jnp.log1p(jnp.exp(-jnp.abs(z)))


def _sigmoid(z):
    return 1.0 / (1.0 + jnp.exp(-z))


def _split_bf16(x):
    hi = x.astype(BF16)
    lo = (x - hi.astype(F32)).astype(BF16)
    return hi, lo


def _inproj_kernel(x_ref, g_ref, w_ref, o_ref):
    o_ref[...] = _dot(_rms(x_ref[...], g_ref[...]).astype(BF16), w_ref[...])


def _inproj(x, g, w):
    M, D = x.shape
    N = w.shape[1]
    tm = min(512, M)
    return pl.pallas_call(
        _inproj_kernel, grid=(M // tm,),
        in_specs=[pl.BlockSpec((tm, D), lambda i: (i, 0)), pl.BlockSpec((1, D), lambda i: (0, 0)),
                  pl.BlockSpec((D, N), lambda i: (0, 0))],
        out_specs=pl.BlockSpec((tm, N), lambda i: (i, 0)),
        out_shape=jax.ShapeDtypeStruct((M, N), F32),
        compiler_params=_params(("parallel",)), name="inproj")(x, g, w)


EVEN_SPLIT = (2 * ML_QK + 2 * ML_V, SB_W, SB_W, SB_W)


def _inproj_even_kernel(x_ref, g_ref, w_ref, wg_ref, gb_ref, pm_ref, qs_ref, ks_ref, vs_ref, gate_ref, *t_refs):
    hn = _rms(x_ref[...], g_ref[...]).astype(BF16)
    res = _dot(hn, w_ref[...])
    off = 0
    for ref, width in zip((pm_ref, qs_ref, ks_ref, vs_ref), EVEN_SPLIT):
        ref[...] = res[:, off:off + width]
        off += width
    gate_ref[...] = _dot(hn, wg_ref[...]) + gb_ref[...]
    if t_refs:
        kt_ref, vt_ref = t_refs
        kt_ref[0] = res[:, off - 2 * SB_W:off - SB_W].T
        vt_ref[0] = res[:, off - SB_W:off].T


def _inproj_even(x, g, w, wg, gb, *, seq_len=None):
    M, D = x.shape
    N = w.shape[1]
    tm = min(512, M)
    widths = EVEN_SPLIT + (LANES,)
    out_specs = [pl.BlockSpec((tm, wd), lambda i: (i, 0)) for wd in widths]
    out_shape = [jax.ShapeDtypeStruct((M, wd), F32) for wd in widths]
    if seq_len is not None:
        nt = seq_len // tm
        out_specs += [pl.BlockSpec((1, SB_W, tm), lambda i: (i // nt, 0, i % nt))] * 2
        out_shape += [jax.ShapeDtypeStruct((M // seq_len, SB_W, seq_len), F32)] * 2
    return pl.pallas_call(
        _inproj_even_kernel, grid=(M // tm,),
        in_specs=[pl.BlockSpec((tm, D), lambda i: (i, 0)), pl.BlockSpec((1, D), lambda i: (0, 0)),
                  pl.BlockSpec((D, N), lambda i: (0, 0)), pl.BlockSpec((D, LANES), lambda i: (0, 0)),
                  pl.BlockSpec((1, LANES), lambda i: (0, 0))],
        out_specs=out_specs, out_shape=out_shape,
        compiler_params=_params(("parallel",)), name="inproj_even")(x, g, w, wg, gb)


def _inproj_tmajor_kernel(x_ref, g_ref, w_ref, o_ref, ut_ref):
    res = _dot(_rms(x_ref[...], g_ref[...]).astype(BF16), w_ref[...])
    o_ref[...] = res
    ut_ref[...] = res[:, :S5_CH]


def _inproj_tmajor(x, g, w, *, n_seq, T):
    M, D = x.shape
    N = w.shape[1]
    tm = 512
    nt = T // tm
    return pl.pallas_call(
        _inproj_tmajor_kernel, grid=(M // tm,),
        in_specs=[pl.BlockSpec((tm, D), lambda i: (i, 0)), pl.BlockSpec((1, D), lambda i: (0, 0)),
                  pl.BlockSpec((D, N), lambda i: (0, 0))],
        out_specs=[pl.BlockSpec((tm, N), lambda i: (i, 0)),
                   pl.BlockSpec((tm, S5_CH), lambda i: (i % nt, i // nt))],
        out_shape=[jax.ShapeDtypeStruct((M, N), F32), jax.ShapeDtypeStruct((T, n_seq * S5_CH), F32)],
        compiler_params=_params(("parallel",)), name="inproj_tmajor")(x, g, w)


def _mlstm_kernel(q_ref, k_ref, v_ref, o_ref, gt_ref, gh_ref, c0_ref, n0_ref, m0_ref,
                  h_ref, c_out, n_out, m_out, caug_scr, m_scr, *, G, L, CS, last, nc):
    c = pl.program_id(1)
    first_chunk = (lambda f: f()) if nc == 1 else pl.when(c == 0)
    last_chunk = (lambda f: f()) if nc == 1 else pl.when(c == nc - 1)
    row = lax.broadcasted_iota(jnp.int32, (L, L), 0)
    col = lax.broadcasted_iota(jnp.int32, (L, L), 1)
    causal = col <= row
    eye = col == row
    e0 = (lax.broadcasted_iota(jnp.int32, (L, LANES), 1) == 0).astype(F32)
    r64 = lax.broadcasted_iota(jnp.int32, (ML_DK, ML_DK), 0)
    c64 = lax.broadcasted_iota(jnp.int32, (ML_DK, ML_DK), 1)
    eye64 = r64 == c64
    lane_dk = lax.broadcasted_iota(jnp.int32, (ML_DK, LANES), 1)
    lane_row = lax.broadcasted_iota(jnp.int32, (1, LANES), 1)
    rowid = lax.broadcasted_iota(jnp.int32, (L, 1), 0)

    def to_row(colvec):
        return jnp.sum(jnp.where(eye, colvec, 0.0), axis=0, keepdims=True)

    def load_state(g):
        for h in range(ML_HEADS):
            nrow = n0_ref[g, pl.ds(h, 1), :]
            ncol = jnp.sum(jnp.where(eye64, nrow, 0.0), axis=1, keepdims=True)
            caug_scr[g, h] = jnp.concatenate(
                [c0_ref[g, h], jnp.where(lane_dk == 0, ncol, 0.0)], axis=1)
            m_scr[g, h] = jnp.broadcast_to(m0_ref[g][:, h:h + 1], (SUBLANES, LANES))

    def store_state(g, m_row):
        for h in range(ML_HEADS):
            caug_fin = caug_scr[g, h]
            c_out[g, h] = caug_fin[:, :ML_DV]
            ncol = caug_fin[:, ML_DV:ML_DV + 1]
            n_out[g, pl.ds(h, 1), :] = jnp.sum(jnp.where(eye64, ncol, 0.0), axis=0, keepdims=True)
        m_out[g] = m_row

    def chunks(g):
        m_row = jnp.zeros((1, LANES), F32)
        for kk, h in [(kk, h) for kk in range(CS) for h in range(ML_HEADS)]:
            rows = pl.ds(kk * L, L)
            gates = gt_ref[g, rows, :]
            q = q_ref[g, rows, pl.ds(h * ML_DK, ML_DK)]
            kc = k_ref[g, rows, pl.ds(h * ML_DK, ML_DK)] * (ML_DK ** -0.5)
            v = v_ref[g, rows, pl.ds(h * ML_DV, ML_DV)]
            icol = gates[:, h:h + 1]
            fcol = gates[:, ML_HEADS + h:ML_HEADS + h + 1]
            caug = caug_scr[g, h]
            m_prev = m_scr[g, h][0:1, 0:1]

            lfcol = jnp.minimum(fcol, 0.0) - jnp.log1p(jnp.exp(-jnp.abs(fcol)))
            lfrow = to_row(lfcol)
            irow = to_row(icol)
            bcol = jnp.sum(jnp.where(causal, lfrow, 0.0), axis=1, keepdims=True)
            brow = to_row(bcol)
            dmat = jnp.where(causal, bcol - brow + irow, -jnp.inf)
            inter = bcol + m_prev
            mt = jnp.maximum(inter, jnp.max(dmat, axis=1, keepdims=True))
            w_inter = jnp.exp(inter - mt)
            qb = q.astype(BF16)
            s = _dot_nt(qb, kc.astype(BF16)) * jnp.exp(dmat - mt)
            vaug = jnp.concatenate([v, e0], axis=1).astype(BF16)
            tot = w_inter * _dot(qb, caug.astype(BF16)) + _dot(s.astype(BF16), vaug)
            num = tot[:, :ML_DV]
            den = tot[:, ML_DV:ML_DV + 1]
            hh = num / jnp.maximum(jnp.abs(den), jnp.exp(-mt))

            m_new = mt[last:last + 1, :]
            g_state = jnp.exp(inter[last:last + 1, :] - m_new)
            gin = jnp.exp(bcol[last:last + 1, :] - bcol + icol - m_new)
            if last < L - 1:
                gin = jnp.where(rowid <= last, gin, 0.0)
            caug_new = g_state * caug + _dot_tn((kc * gin).astype(BF16), vaug)
            caug_scr[g, h] = caug_new
            m_scr[g, h] = jnp.broadcast_to(m_new, (SUBLANES, LANES))
            m_row = jnp.where(lane_row == h, m_new, m_row)

            hh = hh * lax.rsqrt(jnp.mean(hh * hh, axis=-1, keepdims=True) + EPS)
            hh = hh * gh_ref[:, pl.ds(h * ML_DV, ML_DV)]
            hh = hh * _sigmoid(o_ref[g, rows, pl.ds(h * ML_DV, ML_DV)])
            h_ref[g, rows, pl.ds(h * ML_DV, ML_DV)] = hh
        return m_row

    if G <= 2:
        @first_chunk
        def _():
            for g in range(G):
                load_state(g)

        m_rows = [chunks(g) for g in range(G)]

        @last_chunk
        def _():
            for g in range(G):
                store_state(g, m_rows[g])
    else:
        assert nc == 1

        def seq_body(g, carry):
            load_state(g)
            store_state(g, chunks(g))
            return carry

        lax.fori_loop(0, G, seq_body, 0, unroll=4)


def _mlstm(pe, gates, g_head, c0, n0, m0, *, n_seq, seq_len, last):
    M = pe.shape[0]
    if seq_len >= ML_CHUNK:
        G, L, CS = 1, ML_CHUNK, 1
    else:
        L, CS = seq_len, 1
        G = ML_CHUNK // L
    nc = seq_len // (L * CS)
    last_l = last - (seq_len // L - 1) * L
    assert CS == 1 or last_l == L - 1
    grid = (n_seq // G, nc)
    R = L * CS
    pe = pe.reshape(n_seq, seq_len, pe.shape[1])
    gates = gates.reshape(n_seq, seq_len, LANES)
    kernel = functools.partial(_mlstm_kernel, G=G, L=L, CS=CS, last=last_l, nc=nc)
    h, c_new, n_new, m_new = pl.pallas_call(
        kernel, grid=grid,
        in_specs=[
            pl.BlockSpec((G, R, ML_QK), lambda b, c: (b, c, 0)),
            pl.BlockSpec((G, R, ML_QK), lambda b, c: (b, c, 1)),
            pl.BlockSpec((G, R, ML_V), lambda b, c: (b, c, 1)),
            pl.BlockSpec((G, R, ML_V), lambda b, c: (b, c, 2)),
            pl.BlockSpec((G, R, LANES), lambda b, c: (b, c, 0)),
            pl.BlockSpec((1, ML_V), lambda b, c: (0, 0)),
            pl.BlockSpec((G, ML_HEADS, ML_DK, ML_DV), lambda b, c: (b, 0, 0, 0)),
            pl.BlockSpec((G, ML_HEADS, ML_DK), lambda b, c: (b, 0, 0)),
            pl.BlockSpec((G, 1, ML_HEADS), lambda b, c: (b, 0, 0)),
        ],
        out_specs=[
            pl.BlockSpec((G, R, ML_V), lambda b, c: (b, c, 0)),
            pl.BlockSpec((G, ML_HEADS, ML_DK, ML_DV), lambda b, c: (b, 0, 0, 0)),
            pl.BlockSpec((G, ML_HEADS, ML_DK), lambda b, c: (b, 0, 0)),
            pl.BlockSpec((G, 1, LANES), lambda b, c: (b, 0, 0)),
        ],
        out_shape=[
            jax.ShapeDtypeStruct((n_seq, seq_len, ML_V), F32),
            jax.ShapeDtypeStruct((n_seq, ML_HEADS, ML_DK, ML_DV), F32),
            jax.ShapeDtypeStruct((n_seq, ML_HEADS, ML_DK), F32),
            jax.ShapeDtypeStruct((n_seq, 1, LANES), F32),
        ],
        scratch_shapes=[pltpu.VMEM((G, ML_HEADS, ML_DK, 2 * ML_DV), F32),
                        pltpu.VMEM((G, ML_HEADS, SUBLANES, LANES), F32)],
        compiler_params=_params(("parallel", "arbitrary")), name="mlstm",
    )(pe, pe, pe, pe, gates, g_head, c0, n0, m0.reshape(n_seq, 1, ML_HEADS))
    return h.reshape(M, ML_V), c_new, n_new, m_new


def _suffix_matrix(n):
    r = lax.broadcasted_iota(jnp.int32, (n, n), 0)
    c = lax.broadcasted_iota(jnp.int32, (n, n), 1)
    return (r > c).astype(BF16)


def _sb_tile(z, valid, lrem, u_mat):
    sp = _softplus(z)
    lk = -sp if valid is None else jnp.where(valid, -sp, 0.0)
    hi, lo = _split_bf16(lk)
    later = _dot(hi, u_mat) + _dot(lo, u_mat)
    w = jnp.exp(z - sp + later + lrem)
    if valid is not None:
        w = jnp.where(valid, w, 0.0)
    return w, lrem + jnp.sum(lk, axis=1, keepdims=True)


def _suffix_rows(x, carry):
    K, Q = x.shape
    sub = lax.broadcasted_iota(jnp.int32, (SUBLANES, Q), 0)
    keep = {step: (sub < SUBLANES - step).astype(F32) for step in (1, 2, 4)}
    outs = [None] * (K // SUBLANES)
    for j in reversed(range(K // SUBLANES)):
        y = x[j * SUBLANES:(j + 1) * SUBLANES, :]
        for step in (1, 2, 4):
            y = y + keep[step] * pltpu.roll(y, SUBLANES - step, 0)
        outs[j] = y + carry
        carry = carry + jnp.broadcast_to(y[0:1, :], (SUBLANES, Q))
    return jnp.concatenate(outs, axis=0), carry


def _sb_prompt_kernel(bias_ref, q_ref, k_ref, v_ref, o_ref, ks_ref, vs_ref, qt_ref, acc_ref, *stage_refs, T):
    z_refs, w_refs = stage_refs[:SB_HEADS], stage_refs[SB_HEADS:]
    _sb_prompt_body(bias_ref, q_ref, k_ref, v_ref, o_ref, ks_ref, vs_ref, qt_ref, acc_ref, z_refs, w_refs, T)


def _sb_prompt_body(bias_ref, q_ref, k_ref, v_ref, o_ref, ks_ref, vs_ref, qt_ref, acc_ref, z_refs, w_refs, T):
    qi = pl.program_id(1)
    lane = lax.broadcasted_iota(jnp.int32, (1, LANES), 1)

    @pl.when(qi == 0)
    def _():
        for h in range(SB_HEADS):
            vs_ref[h] = v_ref[:, pl.ds(h * SB_DH, SB_DH)].astype(BF16)
        for hp in range(SB_HEADS // 2):
            pair = k_ref[:, pl.ds(hp * LANES, LANES)]
            ks_ref[2 * hp] = jnp.where(lane < SB_DH, pair, (lane < SB_DH + 2).astype(F32)).astype(BF16)
            ks_ref[2 * hp + 1] = jnp.where(lane >= SB_DH, pair, (lane < 2).astype(F32)).astype(BF16)

    rowi = lax.broadcasted_iota(jnp.int32, (LANES, T), 0)
    for hp in range(SB_HEADS // 2):
        qt = (q_ref[:, pl.ds(hp * LANES, LANES)] * (SB_DH ** -0.5)).T
        for h, own, r0 in ((2 * hp, rowi < SB_DH, SB_DH), (2 * hp + 1, rowi >= SB_DH, 0)):
            b = jnp.full((LANES, T), bias_ref[h], F32)
            b_hi = b.astype(BF16).astype(F32)
            aug = jnp.where(rowi == r0, b_hi, jnp.where(rowi == r0 + 1, b - b_hi, 0.0))
            qt_ref[h] = jnp.where(own, qt, aug).astype(BF16)

    row = lax.broadcasted_iota(jnp.int32, (T, T), 0)
    col = lax.broadcasted_iota(jnp.int32, (T, T), 1)
    strict = row < col

    def keys(j):
        return pl.ds(pl.multiple_of(jnp.maximum(j, 0) * T, T), T)

    def logits(h, j):
        return _dot(ks_ref[h, keys(j), :], qt_ref[h])

    def weights(zt, lrem, valid):
        sp = jnp.maximum(zt, 0.0) + jnp.log(1.0 + jnp.exp(-jnp.abs(zt)))
        if valid is not None:
            sp = jnp.where(valid, sp, 0.0)
        y, lrem = _suffix_rows(sp, lrem)
        w = jnp.exp(zt - y)
        if valid is not None:
            w = jnp.where(valid, w, 0.0)
        return w.astype(BF16), lrem

    def values(h, j, w):
        return _dot_tn(w, vs_ref[h, keys(j), :])

    lrems = []
    for h in range(SB_HEADS):
        w, lrem = weights(logits(h, qi), jnp.zeros((SUBLANES, T), F32), strict)
        acc_ref[h] = jnp.zeros((T, SB_DH), F32)
        lrems.append(lrem)
        w_refs[h][1] = w
        z_refs[h][0] = logits(h, qi - 1)

    def body(it, lrems):
        j = qi - 1 - it
        cur = lax.rem(it, 2)
        nxt = 1 - cur
        out = []
        for h in range(SB_HEADS):
            acc_ref[h] += values(h, j + 1, w_refs[h][nxt])
            w, lrem = weights(z_refs[h][cur], lrems[h], None)
            w_refs[h][cur] = w
            z_refs[h][nxt] = logits(h, j - 1)
            out.append(lrem)
        return tuple(out)

    lax.fori_loop(0, qi, body, tuple(lrems))
    last = lax.rem(qi + 1, 2)
    o_ref[...] = jnp.concatenate(
        [acc_ref[h] + values(h, 0, w_refs[h][last]) for h in range(SB_HEADS)], axis=1)


def _sb_prompt(q, k, v, sb_bias, *, n_seq, seq_len):
    M = q.shape[0]
    T = SB_TILE
    nq = seq_len // T
    kernel = functools.partial(_sb_prompt_kernel, T=T)
    return pl.pallas_call(
        kernel, grid=(n_seq, nq),
        in_specs=[
            pl.BlockSpec(memory_space=pltpu.SMEM),
            pl.BlockSpec((T, SB_W), lambda b, i: (b * nq + i, 0)),
            pl.BlockSpec((seq_len, SB_W), lambda b, i: (b, 0)),
            pl.BlockSpec((seq_len, SB_W), lambda b, i: (b, 0)),
        ],
        out_specs=pl.BlockSpec((T, SB_W), lambda b, i: (b * nq + i, 0)),
        out_shape=jax.ShapeDtypeStruct((M, SB_W), F32),
        scratch_shapes=[pltpu.VMEM((SB_HEADS, seq_len, LANES), BF16),
                        pltpu.VMEM((SB_HEADS, seq_len, SB_DH), BF16),
                        pltpu.VMEM((SB_HEADS, LANES, T), BF16),
                        pltpu.VMEM((SB_HEADS, T, SB_DH), F32)]
        + [pltpu.VMEM((2, T, T), F32)] * SB_HEADS + [pltpu.VMEM((2, T, T), BF16)] * SB_HEADS,
        compiler_params=_params(("parallel", "arbitrary")), name="sb_prompt",
    )(sb_bias, q, k, v)


def _sb_sample_kernel(pt_ref, bias_ref, qbd_ref, kn_ref, vn_ref, ck_hbm, cv_hbm, o_ref,
                      kbuf, vbuf, sem, acc_ref, lrem_ref, *, P, T, n_pages):
    b, s = pl.program_id(0), pl.program_id(1)
    n_steps = pl.num_programs(1)
    g = b * n_steps + s
    slot = lax.rem(g, 2)

    def page_copies(bb, ss, sl, lookup):
        out = []
        for i in range(P):
            page = pt_ref[bb, n_pages - 1 - (ss * P + i)] if lookup else 0
            out.append(pltpu.make_async_copy(ck_hbm.at[page], kbuf.at[sl, i], sem.at[sl, 0]))
            out.append(pltpu.make_async_copy(cv_hbm.at[page], vbuf.at[sl, i], sem.at[sl, 1]))
        return out

    @pl.when(g == 0)
    def _():
        for c in page_copies(0, 0, 0, True):
            c.start()

    @pl.when(g + 1 < pl.num_programs(0) * n_steps)
    def _():
        wrap = s + 1 == n_steps
        for c in page_copies(jnp.where(wrap, b + 1, b), jnp.where(wrap, 0, s + 1), 1 - slot, True):
            c.start()

    for c in page_copies(b, s, slot, False):
        c.wait()

    R = SB_HEADS * T
    u_mat = _suffix_matrix(LANES)
    qb = (qbd_ref[0] * (SB_DH ** -0.5)).astype(BF16)
    bias = bias_ref[...]

    @pl.when(s == 0)
    def _():
        pad = jnp.zeros((LANES - T, SB_W), F32)
        kn = jnp.concatenate([kn_ref[0], pad], axis=0).astype(BF16)
        vn = jnp.concatenate([vn_ref[0], pad], axis=0).astype(BF16)
        t_of_row = lax.broadcasted_iota(jnp.int32, (R, LANES), 0) % T
        colk = lax.broadcasted_iota(jnp.int32, (R, LANES), 1)
        z = _dot_nt(qb, kn) + bias
        w, lrem = _sb_tile(z, colk < t_of_row, jnp.zeros((R, 1), F32), u_mat)
        acc_ref[...] = _dot(w.astype(BF16), vn)
        lrem_ref[...] = jnp.broadcast_to(lrem, (R, LANES))

    acc = acc_ref[...]
    lrem = lrem_ref[:, 0:1]
    kt = jnp.concatenate([kbuf[slot, i].reshape(SB_W, LANES) for i in range(P)], axis=1).astype(BF16)
    z_all = _dot(qb, kt)
    z = jnp.concatenate([z_all[:, i * LANES:(i + 1) * LANES] for i in range(P)], axis=0)
    z = z + jnp.concatenate([bias] * P, axis=0)
    sp = _softplus(z)
    hi, lo = _split_bf16(-sp)
    later = _dot(hi, u_mat) + _dot(lo, u_mat)
    tot = jnp.sum(sp, axis=1, keepdims=True)
    lrems = []
    for i in range(P):
        lrems.append(lrem)
        lrem = lrem - tot[i * R:(i + 1) * R]
    w = jnp.exp(z - sp + later + jnp.concatenate(lrems, axis=0)).astype(BF16)
    w_all = jnp.concatenate([w[i * R:(i + 1) * R] for i in range(P)], axis=1)
    v = jnp.concatenate([vbuf[slot, i].reshape(SB_W, LANES).T for i in range(P)], axis=0).astype(BF16)
    acc = acc + _dot(w_all, v)
    acc_ref[...] = acc
    lrem_ref[...] = jnp.broadcast_to(lrem, (R, LANES))

    @pl.when(s == pl.num_programs(1) - 1)
    def _():
        o_ref[0] = jnp.concatenate(
            [acc[h * T:(h + 1) * T, h * SB_DH:(h + 1) * SB_DH] for h in range(SB_HEADS)], axis=1)


def _sb_sample(qbd, bias_col, k_new, v_new, cache_k, cache_v, page_table):
    DB, R, _ = qbd.shape
    T = R // SB_HEADS
    n_pages = page_table.shape[1]
    P = PAGES_PER_STEP
    page = cache_k.shape[1]
    assert page == LANES
    ck = cache_k.transpose(0, 2, 3, 1)
    cv = cache_v.transpose(0, 2, 3, 1)

    grid_spec = pltpu.PrefetchScalarGridSpec(
        num_scalar_prefetch=1, grid=(DB, n_pages // P),
        in_specs=[pl.BlockSpec((R, 1), lambda b, s, pt: (0, 0)),
                  pl.BlockSpec((1, R, SB_W), lambda b, s, pt: (b, 0, 0)),
                  pl.BlockSpec((1, T, SB_W), lambda b, s, pt: (b, 0, 0)),
                  pl.BlockSpec((1, T, SB_W), lambda b, s, pt: (b, 0, 0)),
                  pl.BlockSpec(memory_space=pl.ANY), pl.BlockSpec(memory_space=pl.ANY)],
        out_specs=pl.BlockSpec((1, T, SB_W), lambda b, s, pt: (b, 0, 0)),
        scratch_shapes=[pltpu.VMEM((2, P, SB_HEADS, SB_DH, page), F32),
                        pltpu.VMEM((2, P, SB_HEADS, SB_DH, page), F32),
                        pltpu.SemaphoreType.DMA((2, 2)),
                        pltpu.VMEM((R, SB_W), F32), pltpu.VMEM((R, LANES), F32)])
    kernel = functools.partial(_sb_sample_kernel, P=P, T=T, n_pages=n_pages)
    return pl.pallas_call(
        kernel, grid_spec=grid_spec, out_shape=jax.ShapeDtypeStruct((DB, T, SB_W), F32),
        compiler_params=_params(("arbitrary", "arbitrary")), name="sb_sample",
    )(page_table, bias_col, qbd, k_new, v_new, ck, cv)


def _s5_param_kernel(lre_ref, lim_ref, ldt_ref, bre_ref, bim_ref, are_ref, aim_ref, bbre_ref, bbim_ref):
    lre = jnp.minimum(lre_ref[...], -1e-4)
    lim = lim_ref[...]
    dt = jnp.exp(ldt_ref[...])
    mag = jnp.exp(lre * dt)
    ab_re = mag * jnp.cos(lim * dt)
    ab_im = mag * jnp.sin(lim * dt)
    den = lre * lre + lim * lim
    c_re = ((ab_re - 1.0) * lre + ab_im * lim) / den
    c_im = (ab_im * lre - (ab_re - 1.0) * lim) / den
    b_re, b_im = bre_ref[...], bim_ref[...]
    are_ref[...] = ab_re
    aim_ref[...] = ab_im
    bbre_ref[...] = c_re * b_re - c_im * b_im
    bbim_ref[...] = c_re * b_im + c_im * b_re


def _s5_params(lam_re, lam_im, log_dt, b_re, b_im):
    N = S5_N
    col = lambda a: a.reshape(N, 1)
    ldt = jnp.repeat(log_dt, S5_STATE).reshape(N, 1)
    full = lambda shape: pl.BlockSpec(shape, lambda: (0,) * len(shape))
    are, aim, bbre, bbim = pl.pallas_call(
        _s5_param_kernel,
        in_specs=[full((N, 1))] * 3 + [full((N, S5_GROUP))] * 2,
        out_specs=[full((N, 1))] * 2 + [full((N, S5_GROUP))] * 2,
        out_shape=[jax.ShapeDtypeStruct((N, 1), F32)] * 2 + [jax.ShapeDtypeStruct((N, S5_GROUP), F32)] * 2,
        name="s5_params",
    )(col(lam_re), col(lam_im), ldt, b_re.reshape(N, S5_GROUP), b_im.reshape(N, S5_GROUP))
    return are.reshape(1, N), aim.reshape(1, N), bbre, bbim


def _block_diag(blocks):
    G, r, c = blocks.shape
    eye = jnp.eye(G, dtype=blocks.dtype)
    return (blocks[:, :, None, :] * eye[:, None, :, None]).reshape(G * r, G * c)


def _s5_kernel(u_ref, are_ref, aim_ref, bb_ref, cre_ref, cim_ref, d_ref, wg_ref, bg_ref, sre0_ref, sim0_ref,
               y_ref, sre_out, sim_out, bu_scr, xs_scr, st_scr, *, TT, t_last):
    tb = pl.program_id(1)
    N = S5_N
    CH = 512
    BS = SUBLANES

    @pl.when(tb == 0)
    def _():
        st_scr[:, 0:N] = sre0_ref[...]
        st_scr[:, N:2 * N] = sim0_ref[...]

    u = u_ref[...].reshape(TT * BS, S5_CH)
    ub = u.astype(BF16)
    UH, NH = S5_CH // 2, N // 2
    for hf in range(2):
        res = _dot(ub[:, hf * UH:(hf + 1) * UH], bb_ref[hf])
        bu_scr[:, hf * NH:(hf + 1) * NH] = res[:, :NH]
        bu_scr[:, N + hf * NH:N + (hf + 1) * NH] = res[:, NH:]

    for cidx in range(N // CH):
        lo = cidx * CH
        ar = jnp.broadcast_to(are_ref[:, lo:lo + CH], (BS, CH))
        ai = jnp.broadcast_to(aim_ref[:, lo:lo + CH], (BS, CH))

        def step(t, carry):
            xr, xi = carry
            r = pl.ds(pl.multiple_of(t * BS, BS), BS)
            br = bu_scr[r, lo:lo + CH]
            bi = bu_scr[r, N + lo:N + lo + CH]
            nr = ar * xr - ai * xi + br
            ni = ar * xi + ai * xr + bi
            xs_scr[r, lo:lo + CH] = nr
            xs_scr[r, N + lo:N + lo + CH] = ni
            return nr, ni

        xr, xi = lax.fori_loop(0, TT, step, (st_scr[:, lo:lo + CH], st_scr[:, N + lo:N + lo + CH]))
        st_scr[:, lo:lo + CH] = xr
        st_scr[:, N + lo:N + lo + CH] = xi

    @pl.when(tb == t_last // TT)
    def _():
        r = pl.ds((t_last % TT) * BS, BS)
        sre_out[...] = xs_scr[r, 0:N]
        sim_out[...] = xs_scr[r, N:2 * N]

    ys = []
    for hf in range(2):
        x_re = xs_scr[:, hf * NH:(hf + 1) * NH].astype(BF16)
        x_im = xs_scr[:, N + hf * NH:N + (hf + 1) * NH].astype(BF16)
        ys.append(_dot(x_re, cre_ref[hf]) - _dot(x_im, cim_ref[hf]))
    y = jnp.concatenate(ys, axis=1) + d_ref[...] * u
    y = jax.nn.gelu(y)
    y = y * _sigmoid(_dot(y.astype(BF16), wg_ref[...]) + bg_ref[...])
    y_ref[...] = y.reshape(TT, BS, S5_CH)


def _s5(u_t, are, aim, bb, cre, cim, d, wg, bg, sre0, sim0, *, t_last):
    T, B, _ = u_t.shape
    TT = min(S5_TT, T)
    N = S5_N
    const = lambda shape: pl.BlockSpec(shape, lambda b, t: (0,) * len(shape))
    kernel = functools.partial(_s5_kernel, TT=TT, t_last=t_last)
    return pl.pallas_call(
        kernel, grid=(B // SUBLANES, T // TT),
        in_specs=[pl.BlockSpec((TT, SUBLANES, S5_CH), lambda b, t: (t, b, 0)),
                  const((1, N)), const((1, N)), const((2, S5_CH // 2, N)), const((2, N // 2, S5_CH // 2)),
                  const((2, N // 2, S5_CH // 2)),
                  const((1, S5_CH)), const((S5_CH, S5_CH)), const((1, S5_CH)),
                  pl.BlockSpec((SUBLANES, N), lambda b, t: (b, 0)),
                  pl.BlockSpec((SUBLANES, N), lambda b, t: (b, 0))],
        out_specs=[pl.BlockSpec((TT, SUBLANES, S5_CH), lambda b, t: (t, b, 0)),
                   pl.BlockSpec((SUBLANES, N), lambda b, t: (b, 0)),
                   pl.BlockSpec((SUBLANES, N), lambda b, t: (b, 0))],
        out_shape=[jax.ShapeDtypeStruct((T, B, S5_CH), F32),
                   jax.ShapeDtypeStruct((B, N), F32), jax.ShapeDtypeStruct((B, N), F32)],
        scratch_shapes=[pltpu.VMEM((TT * SUBLANES, 2 * N), F32), pltpu.VMEM((TT * SUBLANES, 2 * N), F32),
                        pltpu.VMEM((SUBLANES, 2 * N), F32)],
        compiler_params=_params(("parallel", "arbitrary")), name="s5",
    )(u_t, are, aim, bb, cre, cim, d, wg, bg, sre0, sim0)


def _alibi_slope(h):
    return float(2.0 ** (-8.0 * (h + 1) / SW_HEADS))


def _sink_softmax_parts(zs, sink):
    m = sink
    for z in zs:
        m = jnp.maximum(m, jnp.max(z, axis=-1, keepdims=True))
    es = [jnp.exp(z - m) for z in zs]
    tot = jnp.exp(sink - m)
    for e in es:
        tot = tot + jnp.sum(e, axis=-1, keepdims=True)
    return [e / tot for e in es]


def _swa_prompt_kernel(sink_ref, q_ref, kp_ref, kc_ref, vp_ref, vc_ref, o_ref):
    i = pl.program_id(1)
    W = WINDOW
    rowq = lax.broadcasted_iota(jnp.int32, (W, 2 * W), 0)
    colc = lax.broadcasted_iota(jnp.int32, (W, 2 * W), 1)
    dist = W + rowq - colc
    valid = (dist >= 0) & (dist < W) & (colc >= jnp.where(i > 0, 0, W))
    distf = dist.astype(F32)
    outs = []
    for kv in range(SW_KV_HEADS):
        ls = pl.ds(kv * SW_DH, SW_DH)
        kband = jnp.concatenate([kp_ref[:, ls], kc_ref[:, ls]], axis=0).astype(BF16)
        vband = jnp.concatenate([vp_ref[:, ls], vc_ref[:, ls]], axis=0).astype(BF16)
        for gq in range(SW_GROUP):
            h = kv * SW_GROUP + gq
            qh = q_ref[:, pl.ds(h * SW_DH, SW_DH)].astype(BF16)
            z = _dot_nt(qh, kband) * (SW_DH ** -0.5) - _alibi_slope(h) * distf
            z = jnp.where(valid, z, -jnp.inf)
            (p,) = _sink_softmax_parts([z], sink_ref[h])
            outs.append(_dot(p.astype(BF16), vband))
    o_ref[...] = jnp.concatenate(outs, axis=1)


def _swa_prompt(po, sinks, *, n_seq, seq_len):
    M = po.shape[0]
    W = WINDOW
    nb = seq_len // W
    kcol, vcol = (S5_CH + SW_Q) // SW_KV, (S5_CH + SW_Q) // SW_KV + 1
    cur = lambda b, i: b * nb + i
    prev = lambda b, i: b * nb + jnp.maximum(i - 1, 0)
    return pl.pallas_call(
        _swa_prompt_kernel, grid=(n_seq, nb),
        in_specs=[pl.BlockSpec(memory_space=pltpu.SMEM),
                  pl.BlockSpec((W, SW_Q), lambda b, i: (cur(b, i), 1)),
                  pl.BlockSpec((W, SW_KV), lambda b, i: (prev(b, i), kcol)),
                  pl.BlockSpec((W, SW_KV), lambda b, i: (cur(b, i), kcol)),
                  pl.BlockSpec((W, SW_KV), lambda b, i: (prev(b, i), vcol)),
                  pl.BlockSpec((W, SW_KV), lambda b, i: (cur(b, i), vcol))],
        out_specs=pl.BlockSpec((W, SW_Q), lambda b, i: (cur(b, i), 0)),
        out_shape=jax.ShapeDtypeStruct((M, SW_Q), F32),
        compiler_params=_params(("parallel", "arbitrary")), name="swa_prompt",
    )(sinks, po, po, po, po, po)


def _swa_sample_kernel(sink_ref, q_ref, kn_ref, vn_ref, ck_ref, cv_ref, o_ref, *, G, T):
    W = WINDOW
    R = SW_GROUP * T
    t_c = lax.broadcasted_iota(jnp.int32, (R, W), 0) % T
    j_c = lax.broadcasted_iota(jnp.int32, (R, W), 1)
    dist_c = W + t_c - j_c
    valid_c = dist_c < W
    t_n = lax.broadcasted_iota(jnp.int32, (R, T), 0) % T
    s_n = lax.broadcasted_iota(jnp.int32, (R, T), 1)
    dist_n = t_n - s_n
    valid_n = dist_n >= 0
    g_of_row = lax.broadcasted_iota(jnp.int32, (R, 1), 0) // T

    def seq_body(g, carry):
        rows = pl.ds(pl.multiple_of(g * T, T), T)
        outs = []
        for kv in range(SW_KV_HEADS):
            ls = pl.ds(kv * SW_DH, SW_DH)
            slope = jnp.zeros((R, 1), F32)
            sink = jnp.zeros((R, 1), F32)
            for gq in range(SW_GROUP):
                h = kv * SW_GROUP + gq
                slope = jnp.where(g_of_row == gq, _alibi_slope(h), slope)
                sink = jnp.where(g_of_row == gq, sink_ref[h], sink)
            q4 = jnp.concatenate(
                [q_ref[rows, pl.ds((kv * SW_GROUP + gq) * SW_DH, SW_DH)] for gq in range(SW_GROUP)],
                axis=0).astype(BF16)
            kc = ck_ref[g, :, ls].astype(BF16)
            vc = cv_ref[g, :, ls].astype(BF16)
            kn = kn_ref[rows, ls].astype(BF16)
            vn = vn_ref[rows, ls].astype(BF16)
            zc = _dot_nt(q4, kc) * (SW_DH ** -0.5) - slope * dist_c.astype(F32)
            zn = _dot_nt(q4, kn) * (SW_DH ** -0.5) - slope * dist_n.astype(F32)
            zc = jnp.where(valid_c, zc, -jnp.inf)
            zn = jnp.where(valid_n, zn, -jnp.inf)
            pc, pn = _sink_softmax_parts([zc, zn], sink)
            o4 = _dot(pc.astype(BF16), vc) + _dot(pn.astype(BF16), vn)
            outs += [o4[gq * T:(gq + 1) * T] for gq in range(SW_GROUP)]
        o_ref[rows, :] = jnp.concatenate(outs, axis=1)
        return carry

    lax.fori_loop(0, G, seq_body, 0, unroll=4)


def _swa_sample(po, sinks, cache_k, cache_v, *, n_seq, T):
    M = po.shape[0]
    G = 16
    kcol, vcol = (S5_CH + SW_Q) // SW_KV, (S5_CH + SW_Q) // SW_KV + 1
    kernel = functools.partial(_swa_sample_kernel, G=G, T=T)
    return pl.pallas_call(
        kernel, grid=(n_seq // G,),
        in_specs=[pl.BlockSpec(memory_space=pltpu.SMEM),
                  pl.BlockSpec((G * T, SW_Q), lambda b: (b, 1)),
                  pl.BlockSpec((G * T, SW_KV), lambda b: (b, kcol)),
                  pl.BlockSpec((G * T, SW_KV), lambda b: (b, vcol)),
                  pl.BlockSpec((G, WINDOW, SW_KV), lambda b: (b, 0, 0)),
                  pl.BlockSpec((G, WINDOW, SW_KV), lambda b: (b, 0, 0))],
        out_specs=pl.BlockSpec((G * T, SW_Q), lambda b: (b, 0)),
        out_shape=jax.ShapeDtypeStruct((M, SW_Q), F32),
        compiler_params=_params(("parallel",)), name="swa_sample",
    )(sinks, po, po, po, cache_k, cache_v)


def _post_kernel(x_ref, a_ref, b_ref, wa_ref, wb_ref, g_ref, w1_ref, w2_ref, gf_ref, o_ref,
                 x1_scr, hn_scr, acc_scr, *, final):
    j = pl.program_id(1)

    @pl.when(j == 0)
    def _():
        x1 = x_ref[...] + _dot(a_ref[...].astype(BF16), wa_ref[...]) + _dot(b_ref[...].astype(BF16), wb_ref[...])
        x1_scr[...] = x1
        hn_scr[...] = _rms(x1, g_ref[...]).astype(BF16)
        acc_scr[...] = jnp.zeros_like(acc_scr)

    hmid = jnp.square(jnp.maximum(_dot(hn_scr[...], w1_ref[...]), 0.0))
    acc_scr[...] += _dot(hmid.astype(BF16), w2_ref[...])

    @pl.when(j == pl.num_programs(1) - 1)
    def _():
        y = x1_scr[...] + acc_scr[...]
        if final:
            y = _rms(y, gf_ref[...])
        o_ref[...] = y


def _post(x, a, b, wa, wb, g, w1, w2, gf, *, final, a_tmajor_seq_len=None):
    M, D = x.shape
    tm = min(512, M)
    tf = 2048
    Ka, Kb = wa.shape[0], wb.shape[0]
    if a_tmajor_seq_len is None:
        a_map = lambda i, j: (i, 0)
    else:
        nt = a_tmajor_seq_len // tm
        a_map = lambda i, j: (i % nt, i // nt)
    kernel = functools.partial(_post_kernel, final=final)
    return pl.pallas_call(
        kernel, grid=(M // tm, D_FF // tf),
        in_specs=[pl.BlockSpec((tm, D), lambda i, j: (i, 0)),
                  pl.BlockSpec((tm, Ka), a_map),
                  pl.BlockSpec((tm, Kb), lambda i, j: (i, 0)),
                  pl.BlockSpec((Ka, D), lambda i, j: (0, 0)),
                  pl.BlockSpec((Kb, D), lambda i, j: (0, 0)),
                  pl.BlockSpec((1, D), lambda i, j: (0, 0)),
                  pl.BlockSpec((D, tf), lambda i, j: (0, j)),
                  pl.BlockSpec((tf, D), lambda i, j: (j, 0)),
                  pl.BlockSpec((1, D), lambda i, j: (0, 0))],
        out_specs=pl.BlockSpec((tm, D), lambda i, j: (i, 0)),
        out_shape=jax.ShapeDtypeStruct((M, D), F32),
        scratch_shapes=[pltpu.VMEM((tm, D), F32), pltpu.VMEM((tm, D), BF16), pltpu.VMEM((tm, D), F32)],
        compiler_params=_params(("parallel", "arbitrary")), name="post",
    )(x, a, b, wa, wb, g, w1, w2, gf)


def _trunk(x, ml_state, s5_state, sb_fn, swa_fn, p, *, n_seq, T, t_real):
    last = t_real - 1
    tmajor = T % 512 == 0
    pm, q_sb, k_sb, v_sb, gates, *kv_t = _inproj_even(
        x, p["g_mix0"], p["w_even"], p["w_gate"], p["b_gate"], seq_len=T if tmajor else None)
    if tmajor:
        k_state, v_state = (a.reshape(n_seq, SB_HEADS, SB_DH, T).transpose(0, 3, 1, 2) for a in kv_t)
    else:
        k_state, v_state = (a.reshape(n_seq, T, SB_HEADS, SB_DH) for a in (k_sb, v_sb))
    h_ml, c_new, n_new, m_new = _mlstm(pm, gates, p["g_head"], *ml_state, n_seq=n_seq, seq_len=T, last=last)
    h_sb = sb_fn(q_sb, k_sb, v_sb)
    x = _post(x, h_ml, h_sb, p["wo_even_a"], p["wo_even_b"], p["g_ffn0"], p["w1_0"], p["w2_0"],
              p["g_final"], final=False)
    if tmajor:
        po, u_t = _inproj_tmajor(x, p["g_mix1"], p["w_odd"], n_seq=n_seq, T=T)
        u_t = u_t.reshape(T, n_seq, S5_CH)
    else:
        po = _inproj(x, p["g_mix1"], p["w_odd"])
        u_t = po[:, :S5_CH].reshape(n_seq, T, S5_CH).transpose(1, 0, 2)
    y_t, s_re, s_im = _s5(u_t, p["a_re"], p["a_im"], p["bb"], p["cc_re"], p["cc_im"], p["s5_d"], p["w_glu"],
                          p["b_glu"], s5_state[0].reshape(n_seq, S5_N), s5_state[1].reshape(n_seq, S5_N),
                          t_last=last)
    o_sw = swa_fn(po)
    if tmajor:
        y_s5, seq_len = y_t.reshape(T, n_seq * S5_CH), T
    else:
        y_s5, seq_len = y_t.transpose(1, 0, 2).reshape(n_seq * T, S5_CH), None
    x = _post(x, y_s5, o_sw, p["wo_odd_a"], p["wo_odd_b"], p["g_ffn1"], p["w1_1"], p["w2_1"],
              p["g_final"], final=True, a_tmajor_seq_len=seq_len)
    state = (c_new, n_new, m_new[:, 0, :ML_HEADS], k_state, v_state, s_re.reshape(n_seq, S5_GROUPS, S5_STATE),
             s_im.reshape(n_seq, S5_GROUPS, S5_STATE), po)
    return x, state


def kernel(x_prompt, x_sample, state_mlstm_C, state_mlstm_n, state_mlstm_m, cache_sb_k, cache_sb_v, page_table, state_s5_re, state_s5_im, cache_swa_k, cache_swa_v, g_norm_mix, g_norm_ffn, g_norm_final, w_in_even, b_igate, b_fgate, g_mlstm_head, sb_bias, w_out_even, w_in_odd, s5_lambda_re, s5_lambda_im, s5_log_dt, s5_B_re, s5_B_im, s5_C_re, s5_C_im, s5_D, w_glu, b_glu, swa_sinks, w_out_odd, w_ff1, w_ff2):
    Bp, S, D = x_prompt.shape
    DB, Ts, _ = x_sample.shape
    n_gate = 2 * ML_HEADS
    g0 = 2 * ML_QK + 2 * ML_V
    row = lambda a: a.reshape(1, -1).astype(F32)

    a_re, a_im, bb_re, bb_im = _s5_params(s5_lambda_re, s5_lambda_im, s5_log_dt, s5_B_re, s5_B_im)
    bbt = lambda bb: _block_diag(bb.reshape(S5_GROUPS, S5_STATE, S5_GROUP).transpose(0, 2, 1))
    half_blocks = lambda m, hf: m[hf * m.shape[0] // 2:(hf + 1) * m.shape[0] // 2,
                                  hf * m.shape[1] // 2:(hf + 1) * m.shape[1] // 2]
    p = {
        "g_mix0": row(g_norm_mix[0]), "g_mix1": row(g_norm_mix[1]),
        "g_ffn0": row(g_norm_ffn[0]), "g_ffn1": row(g_norm_ffn[1]), "g_final": row(g_norm_final),
        "w_even": jnp.concatenate([w_in_even[:, :g0], w_in_even[:, g0 + n_gate:]], axis=1).astype(BF16),
        "w_gate": jnp.pad(w_in_even[:, g0:g0 + n_gate], ((0, 0), (0, LANES - n_gate))).astype(BF16),
        "b_gate": jnp.pad(jnp.concatenate([b_igate, b_fgate]), (0, LANES - n_gate)).reshape(1, LANES).astype(F32),
        "g_head": row(g_mlstm_head),
        "wo_even_a": w_out_even[:ML_V].astype(BF16), "wo_even_b": w_out_even[ML_V:].astype(BF16),
        "w_odd": w_in_odd.astype(BF16),
        "a_re": a_re, "a_im": a_im,
        "bb": jnp.stack([jnp.concatenate([half_blocks(bbt(bb_re), hf), half_blocks(bbt(bb_im), hf)], axis=1)
                         for hf in range(2)]).astype(BF16),
        "cc_re": jnp.stack([half_blocks(_block_diag(s5_C_re.transpose(0, 2, 1)), hf)
                            for hf in range(2)]).astype(BF16),
        "cc_im": jnp.stack([half_blocks(_block_diag(s5_C_im.transpose(0, 2, 1)), hf)
                            for hf in range(2)]).astype(BF16),
        "s5_d": row(s5_D), "w_glu": w_glu.astype(BF16), "b_glu": row(b_glu),
        "wo_odd_a": w_out_odd[:S5_CH].astype(BF16), "wo_odd_b": w_out_odd[S5_CH:].astype(BF16),
        "w1_0": w_ff1[0].astype(BF16), "w2_0": w_ff2[0].astype(BF16),
        "w1_1": w_ff1[1].astype(BF16), "w2_1": w_ff2[1].astype(BF16),
    }
    sb_bias = sb_bias.astype(F32)
    sinks = swa_sinks.astype(F32)

    ml0 = (jnp.zeros((Bp, ML_HEADS, ML_DK, ML_DV), F32), jnp.zeros((Bp, ML_HEADS, ML_DK), F32),
           jnp.zeros((Bp, ML_HEADS), F32))
    s50 = (jnp.zeros((Bp, S5_GROUPS, S5_STATE), F32), jnp.zeros((Bp, S5_GROUPS, S5_STATE), F32))
    y_p, (c_p, n_p, m_p, ksb_p, vsb_p, sre_p, sim_p, po_p) = _trunk(
        x_prompt.reshape(Bp * S, D), ml0, s50,
        functools.partial(_sb_prompt, sb_bias=sb_bias, n_seq=Bp, seq_len=S),
        functools.partial(_swa_prompt, sinks=sinks, n_seq=Bp, seq_len=S),
        p, n_seq=Bp, T=S, t_real=S)
    sb_k_p, sb_v_p = ksb_p, vsb_p
    po3 = po_p.reshape(Bp, S, -1)
    swa_k_p = po3[:, -WINDOW:, S5_CH + SW_Q:S5_CH + SW_Q + SW_KV].reshape(Bp, WINDOW, SW_KV_HEADS, SW_DH)
    swa_v_p = po3[:, -WINDOW:, S5_CH + SW_Q + SW_KV:].reshape(Bp, WINDOW, SW_KV_HEADS, SW_DH)

    T = T_PAD
    xs = jnp.pad(x_sample, ((0, 0), (0, T - Ts), (0, 0))).reshape(DB * T, D)
    bias_col = jnp.repeat(sb_bias, T).reshape(SB_HEADS * T, 1)
    head_eye = jnp.eye(SB_HEADS, dtype=F32)

    def sb_sample_fn(q_sb, k_sb, v_sb):
        q = q_sb.reshape(DB, T, SB_HEADS, SB_DH)
        qbd = (q.transpose(0, 2, 1, 3)[:, :, :, None, :] * head_eye[None, :, None, :, None]).reshape(
            DB, SB_HEADS * T, SB_W)
        out = _sb_sample(qbd, bias_col, k_sb.reshape(DB, T, SB_W), v_sb.reshape(DB, T, SB_W),
                         cache_sb_k, cache_sb_v, page_table)
        return out.reshape(DB * T, SB_W)

    ck = cache_swa_k.reshape(DB, WINDOW, SW_KV)
    cv = cache_swa_v.reshape(DB, WINDOW, SW_KV)
    y_s, (c_s, n_s, m_s, ksb_s, vsb_s, sre_s, sim_s, po_s) = _trunk(
        xs, (state_mlstm_C, state_mlstm_n, state_mlstm_m), (state_s5_re, state_s5_im),
        sb_sample_fn, functools.partial(_swa_sample, sinks=sinks, cache_k=ck, cache_v=cv, n_seq=DB, T=T),
        p, n_seq=DB, T=T, t_real=Ts)
    y_s = y_s.reshape(DB, T, D)[:, :Ts]
    sb_k_s, sb_v_s = ksb_s[:, :Ts], vsb_s[:, :Ts]
    po_s3 = po_s.reshape(DB, T, -1)[:, :Ts]
    k_new = po_s3[:, :, S5_CH + SW_Q:S5_CH + SW_Q + SW_KV].reshape(DB, Ts, SW_KV_HEADS, SW_DH)
    v_new = po_s3[:, :, S5_CH + SW_Q + SW_KV:].reshape(DB, Ts, SW_KV_HEADS, SW_DH)
    swa_k_s = jnp.concatenate([cache_swa_k[:, Ts:], k_new], axis=1)
    swa_v_s = jnp.concatenate([cache_swa_v[:, Ts:], v_new], axis=1)

    return (y_p.reshape(Bp, S, D), y_s, c_p, n_p, m_p, sb_k_p, sb_v_p, sre_p, sim_p, swa_k_p, swa_v_p,
            c_s, n_s, m_s, sb_k_s, sb_v_s, sre_s, sim_s, swa_k_s, swa_v_s)
```

```python
import functools
import math

import jax
import jax.numpy as jnp
import numpy as np
from jax import lax
from jax.experimental import pallas as pl
from jax.experimental.pallas import tpu as pltpu

F32 = jnp.float32
BF16 = jnp.bfloat16

D_MODEL = 1024
PAST_LEN = 8192
ML_HEADS, ML_DK, ML_DV = 4, 64, 128
ML_QK, ML_V = ML_HEADS * ML_DK, ML_HEADS * ML_DV
SB_HEADS, SB_DH = 8, 64
SB_W = SB_HEADS * SB_DH
S5_CH, S5_GROUP, S5_GROUPS, S5_STATE = 512, 16, 32, 64
S5_N = S5_GROUPS * S5_STATE
SW_HEADS, SW_KV_HEADS, SW_DH = 8, 2, 64
SW_Q, SW_KV = SW_HEADS * SW_DH, SW_KV_HEADS * SW_DH
SW_GROUP = SW_HEADS // SW_KV_HEADS
WINDOW = 128
D_FF = 4 * D_MODEL
EPS = 1e-6

LANES = 128
SUBLANES = 8
VMEM_LIMIT = 56 * 1024 * 1024
T_PAD = SUBLANES
ML_CHUNK = 128
SB_TILE = 128
PAGES_PER_STEP = 32
S5_TT = 32


def _params(sem):
    return pltpu.CompilerParams(dimension_semantics=sem, vmem_limit_bytes=VMEM_LIMIT)


def _dot(a, b):
    return jnp.dot(a, b, preferred_element_type=F32)


def _dot_nt(a, b):
    return lax.dot_general(a, b, (((1,), (1,)), ((), ())), preferred_element_type=F32)


def _dot_tn(a, b):
    return lax.dot_general(a, b, (((0,), (0,)), ((), ())), preferred_element_type=F32)


def _rms(x, g):
    ms = jnp.mean(x * x, axis=-1, keepdims=True)
    return x * lax.rsqrt(ms + EPS) * g


def _softplus(z):
    return jnp.maximum(z, 0.0) + jnp.log1p(jnp.exp(-jnp.abs(z)))


def _sigmoid(z):
    return 1.0 / (1.0 + jnp.exp(-z))


def _split_bf16(x):
    hi = x.astype(BF16)
    lo = (x - hi.astype(F32)).astype(BF16)
    return hi, lo


def _inproj_kernel(x_ref, g_ref, w_ref, o_ref):
    o_ref[...] = _dot(_rms(x_ref[...], g_ref[...]).astype(BF16), w_ref[...])


def _inproj(x, g, w):
    M, D = x.shape
    N = w.shape[1]
    tm = min(512, M)
    return pl.pallas_call(
        _inproj_kernel, grid=(M // tm,),
        in_specs=[pl.BlockSpec((tm, D), lambda i: (i, 0)), pl.BlockSpec((1, D), lambda i: (0, 0)),
                  pl.BlockSpec((D, N), lambda i: (0, 0))],
        out_specs=pl.BlockSpec((tm, N), lambda i: (i, 0)),
        out_shape=jax.ShapeDtypeStruct((M, N), F32),
        compiler_params=_params(("parallel",)), name="inproj")(x, g, w)


EVEN_SPLIT = (2 * ML_QK + 2 * ML_V, SB_W, SB_W, SB_W)


def _inproj_even_kernel(x_ref, g_ref, w_ref, wg_ref, gb_ref, pm_ref, qs_ref, ks_ref, vs_ref, gate_ref, *t_refs):
    hn = _rms(x_ref[...], g_ref[...]).astype(BF16)
    res = _dot(hn, w_ref[...])
    off = 0
    for ref, width in zip((pm_ref, qs_ref, ks_ref, vs_ref), EVEN_SPLIT):
        ref[...] = res[:, off:off + width]
        off += width
    gate_ref[...] = _dot(hn, wg_ref[...]) + gb_ref[...]
    if t_refs:
        kt_ref, vt_ref = t_refs
        kt_ref[0] = res[:, off - 2 * SB_W:off - SB_W].T
        vt_ref[0] = res[:, off - SB_W:off].T


def _inproj_even(x, g, w, wg, gb, *, seq_len=None):
    M, D = x.shape
    N = w.shape[1]
    tm = min(512, M)
    widths = EVEN_SPLIT + (LANES,)
    out_specs = [pl.BlockSpec((tm, wd), lambda i: (i, 0)) for wd in widths]
    out_shape = [jax.ShapeDtypeStruct((M, wd), F32) for wd in widths]
    if seq_len is not None:
        nt = seq_len // tm
        out_specs += [pl.BlockSpec((1, SB_W, tm), lambda i: (i // nt, 0, i % nt))] * 2
        out_shape += [jax.ShapeDtypeStruct((M // seq_len, SB_W, seq_len), F32)] * 2
    return pl.pallas_call(
        _inproj_even_kernel, grid=(M // tm,),
        in_specs=[pl.BlockSpec((tm, D), lambda i: (i, 0)), pl.BlockSpec((1, D), lambda i: (0, 0)),
                  pl.BlockSpec((D, N), lambda i: (0, 0)), pl.BlockSpec((D, LANES), lambda i: (0, 0)),
                  pl.BlockSpec((1, LANES), lambda i: (0, 0))],
        out_specs=out_specs, out_shape=out_shape,
        compiler_params=_params(("parallel",)), name="inproj_even")(x, g, w, wg, gb)


def _inproj_tmajor_kernel(x_ref, g_ref, w_ref, o_ref, ut_ref):
    res = _dot(_rms(x_ref[...], g_ref[...]).astype(BF16), w_ref[...])
    o_ref[...] = res
    ut_ref[...] = res[:, :S5_CH]


def _inproj_tmajor(x, g, w, *, n_seq, T):
    M, D = x.shape
    N = w.shape[1]
    tm = 512
    nt = T // tm
    return pl.pallas_call(
        _inproj_tmajor_kernel, grid=(M // tm,),
        in_specs=[pl.BlockSpec((tm, D), lambda i: (i, 0)), pl.BlockSpec((1, D), lambda i: (0, 0)),
                  pl.BlockSpec((D, N), lambda i: (0, 0))],
        out_specs=[pl.BlockSpec((tm, N), lambda i: (i, 0)),
                   pl.BlockSpec((tm, S5_CH), lambda i: (i % nt, i // nt))],
        out_shape=[jax.ShapeDtypeStruct((M, N), F32), jax.ShapeDtypeStruct((T, n_seq * S5_CH), F32)],
        compiler_params=_params(("parallel",)), name="inproj_tmajor")(x, g, w)


def _mlstm_kernel(q_ref, k_ref, v_ref, o_ref, gt_ref, gh_ref, c0_ref, n0_ref, m0_ref,
                  h_ref, c_out, n_out, m_out, caug_scr, m_scr, *, G, L, CS, last, nc):
    c = pl.program_id(1)
    first_chunk = (lambda f: f()) if nc == 1 else pl.when(c == 0)
    last_chunk = (lambda f: f()) if nc == 1 else pl.when(c == nc - 1)
    row = lax.broadcasted_iota(jnp.int32, (L, L), 0)
    col = lax.broadcasted_iota(jnp.int32, (L, L), 1)
    causal = col <= row
    eye = col == row
    e0 = (lax.broadcasted_iota(jnp.int32, (L, LANES), 1) == 0).astype(F32)
    r64 = lax.broadcasted_iota(jnp.int32, (ML_DK, ML_DK), 0)
    c64 = lax.broadcasted_iota(jnp.int32, (ML_DK, ML_DK), 1)
    eye64 = r64 == c64
    lane_dk = lax.broadcasted_iota(jnp.int32, (ML_DK, LANES), 1)
    lane_row = lax.broadcasted_iota(jnp.int32, (1, LANES), 1)
    rowid = lax.broadcasted_iota(jnp.int32, (L, 1), 0)

    def to_row(colvec):
        return jnp.sum(jnp.where(eye, colvec, 0.0), axis=0, keepdims=True)

    def load_state(g):
        for h in range(ML_HEADS):
            nrow = n0_ref[g, pl.ds(h, 1), :]
            ncol = jnp.sum(jnp.where(eye64, nrow, 0.0), axis=1, keepdims=True)
            caug_scr[g, h] = jnp.concatenate(
                [c0_ref[g, h], jnp.where(lane_dk == 0, ncol, 0.0)], axis=1)
            m_scr[g, h] = jnp.broadcast_to(m0_ref[g][:, h:h + 1], (SUBLANES, LANES))

    def store_state(g, m_row):
        for h in range(ML_HEADS):
            caug_fin = caug_scr[g, h]
            c_out[g, h] = caug_fin[:, :ML_DV]
            ncol = caug_fin[:, ML_DV:ML_DV + 1]
            n_out[g, pl.ds(h, 1), :] = jnp.sum(jnp.where(eye64, ncol, 0.0), axis=0, keepdims=True)
        m_out[g] = m_row

    def chunks(g):
        m_row = jnp.zeros((1, LANES), F32)
        for kk, h in [(kk, h) for kk in range(CS) for h in range(ML_HEADS)]:
            rows = pl.ds(kk * L, L)
            gates = gt_ref[g, rows, :]
            q = q_ref[g, rows, pl.ds(h * ML_DK, ML_DK)]
            kc = k_ref[g, rows, pl.ds(h * ML_DK, ML_DK)] * (ML_DK ** -0.5)
            v = v_ref[g, rows, pl.ds(h * ML_DV, ML_DV)]
            icol = gates[:, h:h + 1]
            fcol = gates[:, ML_HEADS + h:ML_HEADS + h + 1]
            caug = caug_scr[g, h]
            m_prev = m_scr[g, h][0:1, 0:1]

            lfcol = jnp.minimum(fcol, 0.0) - jnp.log1p(jnp.exp(-jnp.abs(fcol)))
            lfrow = to_row(lfcol)
            irow = to_row(icol)
            bcol = jnp.sum(jnp.where(causal, lfrow, 0.0), axis=1, keepdims=True)
            brow = to_row(bcol)
            dmat = jnp.where(causal, bcol - brow + irow, -jnp.inf)
            inter = bcol + m_prev
            mt = jnp.maximum(inter, jnp.max(dmat, axis=1, keepdims=True))
            w_inter = jnp.exp(inter - mt)
            qb = q.astype(BF16)
            s = _dot_nt(qb, kc.astype(BF16)) * jnp.exp(dmat - mt)
            vaug = jnp.concatenate([v, e0], axis=1).astype(BF16)
            tot = w_inter * _dot(qb, caug.astype(BF16)) + _dot(s.astype(BF16), vaug)
            num = tot[:, :ML_DV]
            den = tot[:, ML_DV:ML_DV + 1]
            hh = num / jnp.maximum(jnp.abs(den), jnp.exp(-mt))

            m_new = mt[last:last + 1, :]
            g_state = jnp.exp(inter[last:last + 1, :] - m_new)
            gin = jnp.exp(bcol[last:last + 1, :] - bcol + icol - m_new)
            if last < L - 1:
                gin = jnp.where(rowid <= last, gin, 0.0)
            caug_new = g_state * caug + _dot_tn((kc * gin).astype(BF16), vaug)
            caug_scr[g, h] = caug_new
            m_scr[g, h] = jnp.broadcast_to(m_new, (SUBLANES, LANES))
            m_row = jnp.where(lane_row == h, m_new, m_row)

            hh = hh * lax.rsqrt(jnp.mean(hh * hh, axis=-1, keepdims=True) + EPS)
            hh = hh * gh_ref[:, pl.ds(h * ML_DV, ML_DV)]
            hh = hh * _sigmoid(o_ref[g, rows, pl.ds(h * ML_DV, ML_DV)])
            h_ref[g, rows, pl.ds(h * ML_DV, ML_DV)] = hh
        return m_row

    if G <= 2:
        @first_chunk
        def _():
            for g in range(G):
                load_state(g)

        m_rows = [chunks(g) for g in range(G)]

        @last_chunk
        def _():
            for g in range(G):
                store_state(g, m_rows[g])
    else:
        assert nc == 1

        def seq_body(g, carry):
            load_state(g)
            store_state(g, chunks(g))
            return carry

        lax.fori_loop(0, G, seq_body, 0, unroll=4)


def _mlstm(pe, gates, g_head, c0, n0, m0, *, n_seq, seq_len, last):
    M = pe.shape[0]
    if seq_len >= ML_CHUNK:
        G, L, CS = 1, ML_CHUNK, 1
    else:
        L, CS = seq_len, 1
        G = ML_CHUNK // L
    nc = seq_len // (L * CS)
    last_l = last - (seq_len // L - 1) * L
    assert CS == 1 or last_l == L - 1
    grid = (n_seq // G, nc)
    R = L * CS
    pe = pe.reshape(n_seq, seq_len, pe.shape[1])
    gates = gates.reshape(n_seq, seq_len, LANES)
    kernel = functools.partial(_mlstm_kernel, G=G, L=L, CS=CS, last=last_l, nc=nc)
    h, c_new, n_new, m_new = pl.pallas_call(
        kernel, grid=grid,
        in_specs=[
            pl.BlockSpec((G, R, ML_QK), lambda b, c: (b, c, 0)),
            pl.BlockSpec((G, R, ML_QK), lambda b, c: (b, c, 1)),
            pl.BlockSpec((G, R, ML_V), lambda b, c: (b, c, 1)),
            pl.BlockSpec((G, R, ML_V), lambda b, c: (b, c, 2)),
            pl.BlockSpec((G, R, LANES), lambda b, c: (b, c, 0)),
            pl.BlockSpec((1, ML_V), lambda b, c: (0, 0)),
            pl.BlockSpec((G, ML_HEADS, ML_DK, ML_DV), lambda b, c: (b, 0, 0, 0)),
            pl.BlockSpec((G, ML_HEADS, ML_DK), lambda b, c: (b, 0, 0)),
            pl.BlockSpec((G, 1, ML_HEADS), lambda b, c: (b, 0, 0)),
        ],
        out_specs=[
            pl.BlockSpec((G, R, ML_V), lambda b, c: (b, c, 0)),
            pl.BlockSpec((G, ML_HEADS, ML_DK, ML_DV), lambda b, c: (b, 0, 0, 0)),
            pl.BlockSpec((G, ML_HEADS, ML_DK), lambda b, c: (b, 0, 0)),
            pl.BlockSpec((G, 1, LANES), lambda b, c: (b, 0, 0)),
        ],
        out_shape=[
            jax.ShapeDtypeStruct((n_seq, seq_len, ML_V), F32),
            jax.ShapeDtypeStruct((n_seq, ML_HEADS, ML_DK, ML_DV), F32),
            jax.ShapeDtypeStruct((n_seq, ML_HEADS, ML_DK), F32),
            jax.ShapeDtypeStruct((n_seq, 1, LANES), F32),
        ],
        scratch_shapes=[pltpu.VMEM((G, ML_HEADS, ML_DK, 2 * ML_DV), F32),
                        pltpu.VMEM((G, ML_HEADS, SUBLANES, LANES), F32)],
        compiler_params=_params(("parallel", "arbitrary")), name="mlstm",
    )(pe, pe, pe, pe, gates, g_head, c0, n0, m0.reshape(n_seq, 1, ML_HEADS))
    return h.reshape(M, ML_V), c_new, n_new, m_new


def _suffix_matrix(n):
    r = lax.broadcasted_iota(jnp.int32, (n, n), 0)
    c = lax.broadcasted_iota(jnp.int32, (n, n), 1)
    return (r > c).astype(BF16)


def _sb_tile(z, valid, lrem, u_mat):
    sp = _softplus(z)
    lk = -sp if valid is None else jnp.where(valid, -sp, 0.0)
    hi, lo = _split_bf16(lk)
    later = _dot(hi, u_mat) + _dot(lo, u_mat)
    w = jnp.exp(z - sp + later + lrem)
    if valid is not None:
        w = jnp.where(valid, w, 0.0)
    return w, lrem + jnp.sum(lk, axis=1, keepdims=True)


def _suffix_rows(x, carry):
    K, Q = x.shape
    sub = lax.broadcasted_iota(jnp.int32, (SUBLANES, Q), 0)
    keep = {step: (sub < SUBLANES - step).astype(F32) for step in (1, 2, 4)}
    outs = [None] * (K // SUBLANES)
    for j in reversed(range(K // SUBLANES)):
        y = x[j * SUBLANES:(j + 1) * SUBLANES, :]
        for step in (1, 2, 4):
            y = y + keep[step] * pltpu.roll(y, SUBLANES - step, 0)
        outs[j] = y + carry
        carry = carry + jnp.broadcast_to(y[0:1, :], (SUBLANES, Q))
    return jnp.concatenate(outs, axis=0), carry


def _sb_prompt_kernel(bias_ref, q_ref, k_ref, v_ref, o_ref, ks_ref, vs_ref, qt_ref, acc_ref, *stage_refs, T):
    z_refs, w_refs = stage_refs[:SB_HEADS], stage_refs[SB_HEADS:]
    _sb_prompt_body(bias_ref, q_ref, k_ref, v_ref, o_ref, ks_ref, vs_ref, qt_ref, acc_ref, z_refs, w_refs, T)


def _sb_prompt_body(bias_ref, q_ref, k_ref, v_ref, o_ref, ks_ref, vs_ref, qt_ref, acc_ref, z_refs, w_refs, T):
    qi = pl.program_id(1)
    lane = lax.broadcasted_iota(jnp.int32, (1, LANES), 1)

    @pl.when(qi == 0)
    def _():
        for h in range(SB_HEADS):
            vs_ref[h] = v_ref[:, pl.ds(h * SB_DH, SB_DH)].astype(BF16)
        for hp in range(SB_HEADS // 2):
            pair = k_ref[:, pl.ds(hp * LANES, LANES)]
            ks_ref[2 * hp] = jnp.where(lane < SB_DH, pair, (lane < SB_DH + 2).astype(F32)).astype(BF16)
            ks_ref[2 * hp + 1] = jnp.where(lane >= SB_DH, pair, (lane < 2).astype(F32)).astype(BF16)

    rowi = lax.broadcasted_iota(jnp.int32, (LANES, T), 0)
    for hp in range(SB_HEADS // 2):
        qt = (q_ref[:, pl.ds(hp * LANES, LANES)] * (SB_DH ** -0.5)).T
        for h, own, r0 in ((2 * hp, rowi < SB_DH, SB_DH), (2 * hp + 1, rowi >= SB_DH, 0)):
            b = jnp.full((LANES, T), bias_ref[h], F32)
            b_hi = b.astype(BF16).astype(F32)
            aug = jnp.where(rowi == r0, b_hi, jnp.where(rowi == r0 + 1, b - b_hi, 0.0))
            qt_ref[h] = jnp.where(own, qt, aug).astype(BF16)

    row = lax.broadcasted_iota(jnp.int32, (T, T), 0)
    col = lax.broadcasted_iota(jnp.int32, (T, T), 1)
    strict = row < col

    def keys(j):
        return pl.ds(pl.multiple_of(jnp.maximum(j, 0) * T, T), T)

    def logits(h, j):
        return _dot(ks_ref[h, keys(j), :], qt_ref[h])

    def weights(zt, lrem, valid):
        sp = jnp.maximum(zt, 0.0) + jnp.log(1.0 + jnp.exp(-jnp.abs(zt)))
        if valid is not None:
            sp = jnp.where(valid, sp, 0.0)
        y, lrem = _suffix_rows(sp, lrem)
        w = jnp.exp(zt - y)
        if valid is not None:
            w = jnp.where(valid, w, 0.0)
        return w.astype(BF16), lrem

    def values(h, j, w):
        return _dot_tn(w, vs_ref[h, keys(j), :])

    lrems = []
    for h in range(SB_HEADS):
        w, lrem = weights(logits(h, qi), jnp.zeros((SUBLANES, T), F32), strict)
        acc_ref[h] = jnp.zeros((T, SB_DH), F32)
        lrems.append(lrem)
        w_refs[h][1] = w
        z_refs[h][0] = logits(h, qi - 1)

    def body(it, lrems):
        j = qi - 1 - it
        cur = lax.rem(it, 2)
        nxt = 1 - cur
        out = []
        for h in range(SB_HEADS):
            acc_ref[h] += values(h, j + 1, w_refs[h][nxt])
            w, lrem = weights(z_refs[h][cur], lrems[h], None)
            w_refs[h][cur] = w
            z_refs[h][nxt] = logits(h, j - 1)
            out.append(lrem)
        return tuple(out)

    lax.fori_loop(0, qi, body, tuple(lrems))
    last = lax.rem(qi + 1, 2)
    o_ref[...] = jnp.concatenate(
        [acc_ref[h] + values(h, 0, w_refs[h][last]) for h in range(SB_HEADS)], axis=1)


def _sb_prompt(q, k, v, sb_bias, *, n_seq, seq_len):
    M = q.shape[0]
    T = SB_TILE
    nq = seq_len // T
    kernel = functools.partial(_sb_prompt_kernel, T=T)
    return pl.pallas_call(
        kernel, grid=(n_seq, nq),
        in_specs=[
            pl.BlockSpec(memory_space=pltpu.SMEM),
            pl.BlockSpec((T, SB_W), lambda b, i: (b * nq + i, 0)),
            pl.BlockSpec((seq_len, SB_W), lambda b, i: (b, 0)),
            pl.BlockSpec((seq_len, SB_W), lambda b, i: (b, 0)),
        ],
        out_specs=pl.BlockSpec((T, SB_W), lambda b, i: (b * nq + i, 0)),
        out_shape=jax.ShapeDtypeStruct((M, SB_W), F32),
        scratch_shapes=[pltpu.VMEM((SB_HEADS, seq_len, LANES), BF16),
                        pltpu.VMEM((SB_HEADS, seq_len, SB_DH), BF16),
                        pltpu.VMEM((SB_HEADS, LANES, T), BF16),
                        pltpu.VMEM((SB_HEADS, T, SB_DH), F32)]
        + [pltpu.VMEM((2, T, T), F32)] * SB_HEADS + [pltpu.VMEM((2, T, T), BF16)] * SB_HEADS,
        compiler_params=_params(("parallel", "arbitrary")), name="sb_prompt",
    )(sb_bias, q, k, v)


def _sb_sample_kernel(pt_ref, bias_ref, qbd_ref, kn_ref, vn_ref, ck_hbm, cv_hbm, o_ref,
                      kbuf, vbuf, sem, acc_ref, lrem_ref, *, P, T, n_pages):
    b, s = pl.program_id(0), pl.program_id(1)
    n_steps = pl.num_programs(1)
    g = b * n_steps + s
    slot = lax.rem(g, 2)

    def page_copies(bb, ss, sl, lookup):
        out = []
        for i in range(P):
            page = pt_ref[bb, n_pages - 1 - (ss * P + i)] if lookup else 0
            out.append(pltpu.make_async_copy(ck_hbm.at[page], kbuf.at[sl, i], sem.at[sl, 0]))
            out.append(pltpu.make_async_copy(cv_hbm.at[page], vbuf.at[sl, i], sem.at[sl, 1]))
        return out

    @pl.when(g == 0)
    def _():
        for c in page_copies(0, 0, 0, True):
            c.start()

    @pl.when(g + 1 < pl.num_programs(0) * n_steps)
    def _():
        wrap = s + 1 == n_steps
        for c in page_copies(jnp.where(wrap, b + 1, b), jnp.where(wrap, 0, s + 1), 1 - slot, True):
            c.start()

    for c in page_copies(b, s, slot, False):
        c.wait()

    R = SB_HEADS * T
    u_mat = _suffix_matrix(LANES)
    qb = (qbd_ref[0] * (SB_DH ** -0.5)).astype(BF16)
    bias = bias_ref[...]

    @pl.when(s == 0)
    def _():
        pad = jnp.zeros((LANES - T, SB_W), F32)
        kn = jnp.concatenate([kn_ref[0], pad], axis=0).astype(BF16)
        vn = jnp.concatenate([vn_ref[0], pad], axis=0).astype(BF16)
        t_of_row = lax.broadcasted_iota(jnp.int32, (R, LANES), 0) % T
        colk = lax.broadcasted_iota(jnp.int32, (R, LANES), 1)
        z = _dot_nt(qb, kn) + bias
        w, lrem = _sb_tile(z, colk < t_of_row, jnp.zeros((R, 1), F32), u_mat)
        acc_ref[...] = _dot(w.astype(BF16), vn)
        lrem_ref[...] = jnp.broadcast_to(lrem, (R, LANES))

    acc = acc_ref[...]
    lrem = lrem_ref[:, 0:1]
    kt = jnp.concatenate([kbuf[slot, i].reshape(SB_W, LANES) for i in range(P)], axis=1).astype(BF16)
    z_all = _dot(qb, kt)
    z = jnp.concatenate([z_all[:, i * LANES:(i + 1) * LANES] for i in range(P)], axis=0)
    z = z + jnp.concatenate([bias] * P, axis=0)
    sp = _softplus(z)
    hi, lo = _split_bf16(-sp)
    later = _dot(hi, u_mat) + _dot(lo, u_mat)
    tot = jnp.sum(sp, axis=1, keepdims=True)
    lrems = []
    for i in range(P):
        lrems.append(lrem)
        lrem = lrem - tot[i * R:(i + 1) * R]
    w = jnp.exp(z - sp + later + jnp.concatenate(lrems, axis=0)).astype(BF16)
    w_all = jnp.concatenate([w[i * R:(i + 1) * R] for i in range(P)], axis=1)
    v = jnp.concatenate([vbuf[slot, i].reshape(SB_W, LANES).T for i in range(P)], axis=0).astype(BF16)
    acc = acc + _dot(w_all, v)
    acc_ref[...] = acc
    lrem_ref[...] = jnp.broadcast_to(lrem, (R, LANES))

    @pl.when(s == pl.num_programs(1) - 1)
    def _():
        o_ref[0] = jnp.concatenate(
            [acc[h * T:(h + 1) * T, h * SB_DH:(h + 1) * SB_DH] for h in range(SB_HEADS)], axis=1)


def _sb_sample(qbd, bias_col, k_new, v_new, cache_k, cache_v, page_table):
    DB, R, _ = qbd.shape
    T = R // SB_HEADS
    n_pages = page_table.shape[1]
    P = PAGES_PER_STEP
    page = cache_k.shape[1]
    assert page == LANES
    ck = cache_k.transpose(0, 2, 3, 1)
    cv = cache_v.transpose(0, 2, 3, 1)

    grid_spec = pltpu.PrefetchScalarGridSpec(
        num_scalar_prefetch=1, grid=(DB, n_pages // P),
        in_specs=[pl.BlockSpec((R, 1), lambda b, s, pt: (0, 0)),
                  pl.BlockSpec((1, R, SB_W), lambda b, s, pt: (b, 0, 0)),
                  pl.BlockSpec((1, T, SB_W), lambda b, s, pt: (b, 0, 0)),
                  pl.BlockSpec((1, T, SB_W), lambda b, s, pt: (b, 0, 0)),
                  pl.BlockSpec(memory_space=pl.ANY), pl.BlockSpec(memory_space=pl.ANY)],
        out_specs=pl.BlockSpec((1, T, SB_W), lambda b, s, pt: (b, 0, 0)),
        scratch_shapes=[pltpu.VMEM((2, P, SB_HEADS, SB_DH, page), F32),
                        pltpu.VMEM((2, P, SB_HEADS, SB_DH, page), F32),
                        pltpu.SemaphoreType.DMA((2, 2)),
                        pltpu.VMEM((R, SB_W), F32), pltpu.VMEM((R, LANES), F32)])
    kernel = functools.partial(_sb_sample_kernel, P=P, T=T, n_pages=n_pages)
    return pl.pallas_call(
        kernel, grid_spec=grid_spec, out_shape=jax.ShapeDtypeStruct((DB, T, SB_W), F32),
        compiler_params=_params(("arbitrary", "arbitrary")), name="sb_sample",
    )(page_table, bias_col, qbd, k_new, v_new, ck, cv)


def _s5_param_kernel(lre_ref, lim_ref, ldt_ref, bre_ref, bim_ref, are_ref, aim_ref, bbre_ref, bbim_ref):
    lre = jnp.minimum(lre_ref[...], -1e-4)
    lim = lim_ref[...]
    dt = jnp.exp(ldt_ref[...])
    mag = jnp.exp(lre * dt)
    ab_re = mag * jnp.cos(lim * dt)
    ab_im = mag * jnp.sin(lim * dt)
    den = lre * lre + lim * lim
    c_re = ((ab_re - 1.0) * lre + ab_im * lim) / den
    c_im = (ab_im * lre - (ab_re - 1.0) * lim) / den
    b_re, b_im = bre_ref[...], bim_ref[...]
    are_ref[...] = ab_re
    aim_ref[...] = ab_im
    bbre_ref[...] = c_re * b_re - c_im * b_im
    bbim_ref[...] = c_re * b_im + c_im * b_re


def _s5_params(lam_re, lam_im, log_dt, b_re, b_im):
    N = S5_N
    col = lambda a: a.reshape(N, 1)
    ldt = jnp.repeat(log_dt, S5_STATE).reshape(N, 1)
    full = lambda shape: pl.BlockSpec(shape, lambda: (0,) * len(shape))
    are, aim, bbre, bbim = pl.pallas_call(
        _s5_param_kernel,
        in_specs=[full((N, 1))] * 3 + [full((N, S5_GROUP))] * 2,
        out_specs=[full((N, 1))] * 2 + [full((N, S5_GROUP))] * 2,
        out_shape=[jax.ShapeDtypeStruct((N, 1), F32)] * 2 + [jax.ShapeDtypeStruct((N, S5_GROUP), F32)] * 2,
        name="s5_params",
    )(col(lam_re), col(lam_im), ldt, b_re.reshape(N, S5_GROUP), b_im.reshape(N, S5_GROUP))
    return are.reshape(1, N), aim.reshape(1, N), bbre, bbim


def _block_diag(blocks):
    G, r, c = blocks.shape
    eye = jnp.eye(G, dtype=blocks.dtype)
    return (blocks[:, :, None, :] * eye[:, None, :, None]).reshape(G * r, G * c)


def _s5_kernel(u_ref, are_ref, aim_ref, bb_ref, cre_ref, cim_ref, d_ref, wg_ref, bg_ref, sre0_ref, sim0_ref,
               y_ref, sre_out, sim_out, bu_scr, xs_scr, st_scr, *, TT, t_last):
    tb = pl.program_id(1)
    N = S5_N
    CH = 512
    BS = SUBLANES

    @pl.when(tb == 0)
    def _():
        st_scr[:, 0:N] = sre0_ref[...]
        st_scr[:, N:2 * N] = sim0_ref[...]

    u = u_ref[...].reshape(TT * BS, S5_CH)
    ub = u.astype(BF16)
    UH, NH = S5_CH // 2, N // 2
    for hf in range(2):
        res = _dot(ub[:, hf * UH:(hf + 1) * UH], bb_ref[hf])
        bu_scr[:, hf * NH:(hf + 1) * NH] = res[:, :NH]
        bu_scr[:, N + hf * NH:N + (hf + 1) * NH] = res[:, NH:]

    for cidx in range(N // CH):
        lo = cidx * CH
        ar = jnp.broadcast_to(are_ref[:, lo:lo + CH], (BS, CH))
        ai = jnp.broadcast_to(aim_ref[:, lo:lo + CH], (BS, CH))

        def step(t, carry):
            xr, xi = carry
            r = pl.ds(pl.multiple_of(t * BS, BS), BS)
            br = bu_scr[r, lo:lo + CH]
            bi = bu_scr[r, N + lo:N + lo + CH]
            nr = ar * xr - ai * xi + br
            ni = ar * xi + ai * xr + bi
            xs_scr[r, lo:lo + CH] = nr
            xs_scr[r, N + lo:N + lo + CH] = ni
            return nr, ni

        xr, xi = lax.fori_loop(0, TT, step, (st_scr[:, lo:lo + CH], st_scr[:, N + lo:N + lo + CH]))
        st_scr[:, lo:lo + CH] = xr
        st_scr[:, N + lo:N + lo + CH] = xi

    @pl.when(tb == t_last // TT)
    def _():
        r = pl.ds((t_last % TT) * BS, BS)
        sre_out[...] = xs_scr[r, 0:N]
        sim_out[...] = xs_scr[r, N:2 * N]

    ys = []
    for hf in range(2):
        x_re = xs_scr[:, hf * NH:(hf + 1) * NH].astype(BF16)
        x_im = xs_scr[:, N + hf * NH:N + (hf + 1) * NH].astype(BF16)
        ys.append(_dot(x_re, cre_ref[hf]) - _dot(x_im, cim_ref[hf]))
    y = jnp.concatenate(ys, axis=1) + d_ref[...] * u
    y = jax.nn.gelu(y)
    y = y * _sigmoid(_dot(y.astype(BF16), wg_ref[...]) + bg_ref[...])
    y_ref[...] = y.reshape(TT, BS, S5_CH)


def _s5(u_t, are, aim, bb, cre, cim, d, wg, bg, sre0, sim0, *, t_last):
    T, B, _ = u_t.shape
    TT = min(S5_TT, T)
    N = S5_N
    const = lambda shape: pl.BlockSpec(shape, lambda b, t: (0,) * len(shape))
    kernel = functools.partial(_s5_kernel, TT=TT, t_last=t_last)
    return pl.pallas_call(
        kernel, grid=(B // SUBLANES, T // TT),
        in_specs=[pl.BlockSpec((TT, SUBLANES, S5_CH), lambda b, t: (t, b, 0)),
                  const((1, N)), const((1, N)), const((2, S5_CH // 2, N)), const((2, N // 2, S5_CH // 2)),
                  const((2, N // 2, S5_CH // 2)),
                  const((1, S5_CH)), const((S5_CH, S5_CH)), const((1, S5_CH)),
                  pl.BlockSpec((SUBLANES, N), lambda b, t: (b, 0)),
                  pl.BlockSpec((SUBLANES, N), lambda b, t: (b, 0))],
        out_specs=[pl.BlockSpec((TT, SUBLANES, S5_CH), lambda b, t: (t, b, 0)),
                   pl.BlockSpec((SUBLANES, N), lambda b, t: (b, 0)),
                   pl.BlockSpec((SUBLANES, N), lambda b, t: (b, 0))],
        out_shape=[jax.ShapeDtypeStruct((T, B, S5_CH), F32),
                   jax.ShapeDtypeStruct((B, N), F32), jax.ShapeDtypeStruct((B, N), F32)],
        scratch_shapes=[pltpu.VMEM((TT * SUBLANES, 2 * N), F32), pltpu.VMEM((TT * SUBLANES, 2 * N), F32),
                        pltpu.VMEM((SUBLANES, 2 * N), F32)],
        compiler_params=_params(("parallel", "arbitrary")), name="s5",
    )(u_t, are, aim, bb, cre, cim, d, wg, bg, sre0, sim0)


def _alibi_slope(h):
    return float(2.0 ** (-8.0 * (h + 1) / SW_HEADS))


def _sink_softmax_parts(zs, sink):
    m = sink
    for z in zs:
        m = jnp.maximum(m, jnp.max(z, axis=-1, keepdims=True))
    es = [jnp.exp(z - m) for z in zs]
    tot = jnp.exp(sink - m)
    for e in es:
        tot = tot + jnp.sum(e, axis=-1, keepdims=True)
    return [e / tot for e in es]


def _swa_prompt_kernel(sink_ref, q_ref, kp_ref, kc_ref, vp_ref, vc_ref, o_ref):
    i = pl.program_id(1)
    W = WINDOW
    rowq = lax.broadcasted_iota(jnp.int32, (W, 2 * W), 0)
    colc = lax.broadcasted_iota(jnp.int32, (W, 2 * W), 1)
    dist = W + rowq - colc
    valid = (dist >= 0) & (dist < W) & (colc >= jnp.where(i > 0, 0, W))
    distf = dist.astype(F32)
    outs = []
    for kv in range(SW_KV_HEADS):
        ls = pl.ds(kv * SW_DH, SW_DH)
        kband = jnp.concatenate([kp_ref[:, ls], kc_ref[:, ls]], axis=0).astype(BF16)
        vband = jnp.concatenate([vp_ref[:, ls], vc_ref[:, ls]], axis=0).astype(BF16)
        for gq in range(SW_GROUP):
            h = kv * SW_GROUP + gq
            qh = q_ref[:, pl.ds(h * SW_DH, SW_DH)].astype(BF16)
            z = _dot_nt(qh, kband) * (SW_DH ** -0.5) - _alibi_slope(h) * distf
            z = jnp.where(valid, z, -jnp.inf)
            (p,) = _sink_softmax_parts([z], sink_ref[h])
            outs.append(_dot(p.astype(BF16), vband))
    o_ref[...] = jnp.concatenate(outs, axis=1)


def _swa_prompt(po, sinks, *, n_seq, seq_len):
    M = po.shape[0]
    W = WINDOW
    nb = seq_len // W
    kcol, vcol = (S5_CH + SW_Q) // SW_KV, (S5_CH + SW_Q) // SW_KV + 1
    cur = lambda b, i: b * nb + i
    prev = lambda b, i: b * nb + jnp.maximum(i - 1, 0)
    return pl.pallas_call(
        _swa_prompt_kernel, grid=(n_seq, nb),
        in_specs=[pl.BlockSpec(memory_space=pltpu.SMEM),
                  pl.BlockSpec((W, SW_Q), lambda b, i: (cur(b, i), 1)),
                  pl.BlockSpec((W, SW_KV), lambda b, i: (prev(b, i), kcol)),
                  pl.BlockSpec((W, SW_KV), lambda b, i: (cur(b, i), kcol)),
                  pl.BlockSpec((W, SW_KV), lambda b, i: (prev(b, i), vcol)),
                  pl.BlockSpec((W, SW_KV), lambda b, i: (cur(b, i), vcol))],
        out_specs=pl.BlockSpec((W, SW_Q), lambda b, i: (cur(b, i), 0)),
        out_shape=jax.ShapeDtypeStruct((M, SW_Q), F32),
        compiler_params=_params(("parallel", "arbitrary")), name="swa_prompt",
    )(sinks, po, po, po, po, po)


def _swa_sample_kernel(sink_ref, q_ref, kn_ref, vn_ref, ck_ref, cv_ref, o_ref, *, G, T):
    W = WINDOW
    R = SW_GROUP * T
    t_c = lax.broadcasted_iota(jnp.int32, (R, W), 0) % T
    j_c = lax.broadcasted_iota(jnp.int32, (R, W), 1)
    dist_c = W + t_c - j_c
    valid_c = dist_c < W
    t_n = lax.broadcasted_iota(jnp.int32, (R, T), 0) % T
    s_n = lax.broadcasted_iota(jnp.int32, (R, T), 1)
    dist_n = t_n - s_n
    valid_n = dist_n >= 0
    g_of_row = lax.broadcasted_iota(jnp.int32, (R, 1), 0) // T

    def seq_body(g, carry):
        rows = pl.ds(pl.multiple_of(g * T, T), T)
        outs = []
        for kv in range(SW_KV_HEADS):
            ls = pl.ds(kv * SW_DH, SW_DH)
            slope = jnp.zeros((R, 1), F32)
            sink = jnp.zeros((R, 1), F32)
            for gq in range(SW_GROUP):
                h = kv * SW_GROUP + gq
                slope = jnp.where(g_of_row == gq, _alibi_slope(h), slope)
                sink = jnp.where(g_of_row == gq, sink_ref[h], sink)
            q4 = jnp.concatenate(
                [q_ref[rows, pl.ds((kv * SW_GROUP + gq) * SW_DH, SW_DH)] for gq in range(SW_GROUP)],
                axis=0).astype(BF16)
            kc = ck_ref[g, :, ls].astype(BF16)
            vc = cv_ref[g, :, ls].astype(BF16)
            kn = kn_ref[rows, ls].astype(BF16)
            vn = vn_ref[rows, ls].astype(BF16)
            zc = _dot_nt(q4, kc) * (SW_DH ** -0.5) - slope * dist_c.astype(F32)
            zn = _dot_nt(q4, kn) * (SW_DH ** -0.5) - slope * dist_n.astype(F32)
            zc = jnp.where(valid_c, zc, -jnp.inf)
            zn = jnp.where(valid_n, zn, -jnp.inf)
            pc, pn = _sink_softmax_parts([zc, zn], sink)
            o4 = _dot(pc.astype(BF16), vc) + _dot(pn.astype(BF16), vn)
            outs += [o4[gq * T:(gq + 1) * T] for gq in range(SW_GROUP)]
        o_ref[rows, :] = jnp.concatenate(outs, axis=1)
        return carry

    lax.fori_loop(0, G, seq_body, 0, unroll=4)


def _swa_sample(po, sinks, cache_k, cache_v, *, n_seq, T):
    M = po.shape[0]
    G = 16
    kcol, vcol = (S5_CH + SW_Q) // SW_KV, (S5_CH + SW_Q) // SW_KV + 1
    kernel = functools.partial(_swa_sample_kernel, G=G, T=T)
    return pl.pallas_call(
        kernel, grid=(n_seq // G,),
        in_specs=[pl.BlockSpec(memory_space=pltpu.SMEM),
                  pl.BlockSpec((G * T, SW_Q), lambda b: (b, 1)),
                  pl.BlockSpec((G * T, SW_KV), lambda b: (b, kcol)),
                  pl.BlockSpec((G * T, SW_KV), lambda b: (b, vcol)),
                  pl.BlockSpec((G, WINDOW, SW_KV), lambda b: (b, 0, 0)),
                  pl.BlockSpec((G, WINDOW, SW_KV), lambda b: (b, 0, 0))],
        out_specs=pl.BlockSpec((G * T, SW_Q), lambda b: (b, 0)),
        out_shape=jax.ShapeDtypeStruct((M, SW_Q), F32),
        compiler_params=_params(("parallel",)), name="swa_sample",
    )(sinks, po, po, po, cache_k, cache_v)


def _post_kernel(x_ref, a_ref, b_ref, wa_ref, wb_ref, g_ref, w1_ref, w2_ref, gf_ref, o_ref,
                 x1_scr, hn_scr, acc_scr, *, final):
    j = pl.program_id(1)

    @pl.when(j == 0)
    def _():
        x1 = x_ref[...] + _dot(a_ref[...].astype(BF16), wa_ref[...]) + _dot(b_ref[...].astype(BF16), wb_ref[...])
        x1_scr[...] = x1
        hn_scr[...] = _rms(x1, g_ref[...]).astype(BF16)
        acc_scr[...] = jnp.zeros_like(acc_scr)

    hmid = jnp.square(jnp.maximum(_dot(hn_scr[...], w1_ref[...]), 0.0))
    acc_scr[...] += _dot(hmid.astype(BF16), w2_ref[...])

    @pl.when(j == pl.num_programs(1) - 1)
    def _():
        y = x1_scr[...] + acc_scr[...]
        if final:
            y = _rms(y, gf_ref[...])
        o_ref[...] = y


def _post_resident_kernel(x_ref, a_ref, b_ref, wa_ref, wb_ref, g_ref, w1_ref, w2_ref, gf_ref, o_ref, *, final):
    x1 = x_ref[...] + _dot(a_ref[...].astype(BF16), wa_ref[...]) + _dot(b_ref[...].astype(BF16), wb_ref[...])
    hn = _rms(x1, g_ref[...]).astype(BF16)
    hmid = jnp.square(jnp.maximum(_dot(hn, w1_ref[...]), 0.0)).astype(BF16)
    y = x1 + _dot(hmid, w2_ref[...])
    if final:
        y = _rms(y, gf_ref[...])
    o_ref[...] = y


def _post_resident(x, a, b, wa, wb, g, w1, w2, gf, *, final, a_tmajor_seq_len=None):
    M, D = x.shape
    tm = min(512, M)
    Ka, Kb = wa.shape[0], wb.shape[0]
    if a_tmajor_seq_len is None:
        a_map = lambda i: (i, 0)
    else:
        nt = a_tmajor_seq_len // tm
        a_map = lambda i: (i % nt, i // nt)
    const = lambda shape: pl.BlockSpec(shape, lambda i: (0, 0))
    return pl.pallas_call(
        functools.partial(_post_resident_kernel, final=final), grid=(M // tm,),
        in_specs=[pl.BlockSpec((tm, D), lambda i: (i, 0)), pl.BlockSpec((tm, Ka), a_map),
                  pl.BlockSpec((tm, Kb), lambda i: (i, 0)), const((Ka, D)), const((Kb, D)), const((1, D)),
                  const((D, D_FF)), const((D_FF, D)), const((1, D))],
        out_specs=pl.BlockSpec((tm, D), lambda i: (i, 0)),
        out_shape=jax.ShapeDtypeStruct((M, D), F32),
        compiler_params=_params(("parallel",)), name="post_resident",
    )(x, a, b, wa, wb, g, w1, w2, gf)


def _post(x, a, b, wa, wb, g, w1, w2, gf, *, final, a_tmajor_seq_len=None):
    M, D = x.shape
    tm = min(512, M)
    tf = 2048
    Ka, Kb = wa.shape[0], wb.shape[0]
    if a_tmajor_seq_len is None:
        a_map = lambda i, j: (i, 0)
    else:
        nt = a_tmajor_seq_len // tm
        a_map = lambda i, j: (i % nt, i // nt)
    kernel = functools.partial(_post_kernel, final=final)
    return pl.pallas_call(
        kernel, grid=(M // tm, D_FF // tf),
        in_specs=[pl.BlockSpec((tm, D), lambda i, j: (i, 0)),
                  pl.BlockSpec((tm, Ka), a_map),
                  pl.BlockSpec((tm, Kb), lambda i, j: (i, 0)),
                  pl.BlockSpec((Ka, D), lambda i, j: (0, 0)),
                  pl.BlockSpec((Kb, D), lambda i, j: (0, 0)),
                  pl.BlockSpec((1, D), lambda i, j: (0, 0)),
                  pl.BlockSpec((D, tf), lambda i, j: (0, j)),
                  pl.BlockSpec((tf, D), lambda i, j: (j, 0)),
                  pl.BlockSpec((1, D), lambda i, j: (0, 0))],
        out_specs=pl.BlockSpec((tm, D), lambda i, j: (i, 0)),
        out_shape=jax.ShapeDtypeStruct((M, D), F32),
        scratch_shapes=[pltpu.VMEM((tm, D), F32), pltpu.VMEM((tm, D), BF16), pltpu.VMEM((tm, D), F32)],
        compiler_params=_params(("parallel", "arbitrary")), name="post",
    )(x, a, b, wa, wb, g, w1, w2, gf)


def _trunk(x, ml_state, s5_state, sb_fn, swa_fn, p, *, n_seq, T, t_real):
    last = t_real - 1
    tmajor = T % 512 == 0
    pm, q_sb, k_sb, v_sb, gates, *kv_t = _inproj_even(
        x, p["g_mix0"], p["w_even"], p["w_gate"], p["b_gate"], seq_len=T if tmajor else None)
    if tmajor:
        k_state, v_state = (a.reshape(n_seq, SB_HEADS, SB_DH, T).transpose(0, 3, 1, 2) for a in kv_t)
    else:
        k_state, v_state = (a.reshape(n_seq, T, SB_HEADS, SB_DH) for a in (k_sb, v_sb))
    h_ml, c_new, n_new, m_new = _mlstm(pm, gates, p["g_head"], *ml_state, n_seq=n_seq, seq_len=T, last=last)
    h_sb = sb_fn(q_sb, k_sb, v_sb)
    x = _post_resident(x, h_ml, h_sb, p["wo_even_a"], p["wo_even_b"], p["g_ffn0"], p["w1_0"], p["w2_0"],
              p["g_final"], final=False)
    if tmajor:
        po, u_t = _inproj_tmajor(x, p["g_mix1"], p["w_odd"], n_seq=n_seq, T=T)
        u_t = u_t.reshape(T, n_seq, S5_CH)
    else:
        po = _inproj(x, p["g_mix1"], p["w_odd"])
        u_t = po[:, :S5_CH].reshape(n_seq, T, S5_CH).transpose(1, 0, 2)
    y_t, s_re, s_im = _s5(u_t, p["a_re"], p["a_im"], p["bb"], p["cc_re"], p["cc_im"], p["s5_d"], p["w_glu"],
                          p["b_glu"], s5_state[0].reshape(n_seq, S5_N), s5_state[1].reshape(n_seq, S5_N),
                          t_last=last)
    o_sw = swa_fn(po)
    if tmajor:
        y_s5, seq_len = y_t.reshape(T, n_seq * S5_CH), T
    else:
        y_s5, seq_len = y_t.transpose(1, 0, 2).reshape(n_seq * T, S5_CH), None
    x = _post_resident(x, y_s5, o_sw, p["wo_odd_a"], p["wo_odd_b"], p["g_ffn1"], p["w1_1"], p["w2_1"],
              p["g_final"], final=True, a_tmajor_seq_len=seq_len)
    state = (c_new, n_new, m_new[:, 0, :ML_HEADS], k_state, v_state, s_re.reshape(n_seq, S5_GROUPS, S5_STATE),
             s_im.reshape(n_seq, S5_GROUPS, S5_STATE), po)
    return x, state


def kernel(x_prompt, x_sample, state_mlstm_C, state_mlstm_n, state_mlstm_m, cache_sb_k, cache_sb_v, page_table, state_s5_re, state_s5_im, cache_swa_k, cache_swa_v, g_norm_mix, g_norm_ffn, g_norm_final, w_in_even, b_igate, b_fgate, g_mlstm_head, sb_bias, w_out_even, w_in_odd, s5_lambda_re, s5_lambda_im, s5_log_dt, s5_B_re, s5_B_im, s5_C_re, s5_C_im, s5_D, w_glu, b_glu, swa_sinks, w_out_odd, w_ff1, w_ff2):
    Bp, S, D = x_prompt.shape
    DB, Ts, _ = x_sample.shape
    n_gate = 2 * ML_HEADS
    g0 = 2 * ML_QK + 2 * ML_V
    row = lambda a: a.reshape(1, -1).astype(F32)

    a_re, a_im, bb_re, bb_im = _s5_params(s5_lambda_re, s5_lambda_im, s5_log_dt, s5_B_re, s5_B_im)
    bbt = lambda bb: _block_diag(bb.reshape(S5_GROUPS, S5_STATE, S5_GROUP).transpose(0, 2, 1))
    half_blocks = lambda m, hf: m[hf * m.shape[0] // 2:(hf + 1) * m.shape[0] // 2,
                                  hf * m.shape[1] // 2:(hf + 1) * m.shape[1] // 2]
    p = {
        "g_mix0": row(g_norm_mix[0]), "g_mix1": row(g_norm_mix[1]),
        "g_ffn0": row(g_norm_ffn[0]), "g_ffn1": row(g_norm_ffn[1]), "g_final": row(g_norm_final),
        "w_even": jnp.concatenate([w_in_even[:, :g0], w_in_even[:, g0 + n_gate:]], axis=1).astype(BF16),
        "w_gate": jnp.pad(w_in_even[:, g0:g0 + n_gate], ((0, 0), (0, LANES - n_gate))).astype(BF16),
        "b_gate": jnp.pad(jnp.concatenate([b_igate, b_fgate]), (0, LANES - n_gate)).reshape(1, LANES).astype(F32),
        "g_head": row(g_mlstm_head),
        "wo_even_a": w_out_even[:ML_V].astype(BF16), "wo_even_b": w_out_even[ML_V:].astype(BF16),
        "w_odd": w_in_odd.astype(BF16),
        "a_re": a_re, "a_im": a_im,
        "bb": jnp.stack([jnp.concatenate([half_blocks(bbt(bb_re), hf), half_blocks(bbt(bb_im), hf)], axis=1)
                         for hf in range(2)]).astype(BF16),
        "cc_re": jnp.stack([half_blocks(_block_diag(s5_C_re.transpose(0, 2, 1)), hf)
                            for hf in range(2)]).astype(BF16),
        "cc_im": jnp.stack([half_blocks(_block_diag(s5_C_im.transpose(0, 2, 1)), hf)
                            for hf in range(2)]).astype(BF16),
        "s5_d": row(s5_D), "w_glu": w_glu.astype(BF16), "b_glu": row(b_glu),
        "wo_odd_a": w_out_odd[:S5_CH].astype(BF16), "wo_odd_b": w_out_odd[S5_CH:].astype(BF16),
        "w1_0": w_ff1[0].astype(BF16), "w2_0": w_ff2[0].astype(BF16),
        "w1_1": w_ff1[1].astype(BF16), "w2_1": w_ff2[1].astype(BF16),
    }
    sb_bias = sb_bias.astype(F32)
    sinks = swa_sinks.astype(F32)

    ml0 = (jnp.zeros((Bp, ML_HEADS, ML_DK, ML_DV), F32), jnp.zeros((Bp, ML_HEADS, ML_DK), F32),
           jnp.zeros((Bp, ML_HEADS), F32))
    s50 = (jnp.zeros((Bp, S5_GROUPS, S5_STATE), F32), jnp.zeros((Bp, S5_GROUPS, S5_STATE), F32))
    y_p, (c_p, n_p, m_p, ksb_p, vsb_p, sre_p, sim_p, po_p) = _trunk(
        x_prompt.reshape(Bp * S, D), ml0, s50,
        functools.partial(_sb_prompt, sb_bias=sb_bias, n_seq=Bp, seq_len=S),
        functools.partial(_swa_prompt, sinks=sinks, n_seq=Bp, seq_len=S),
        p, n_seq=Bp, T=S, t_real=S)
    sb_k_p, sb_v_p = ksb_p, vsb_p
    po3 = po_p.reshape(Bp, S, -1)
    swa_k_p = po3[:, -WINDOW:, S5_CH + SW_Q:S5_CH + SW_Q + SW_KV].reshape(Bp, WINDOW, SW_KV_HEADS, SW_DH)
    swa_v_p = po3[:, -WINDOW:, S5_CH + SW_Q + SW_KV:].reshape(Bp, WINDOW, SW_KV_HEADS, SW_DH)

    T = T_PAD
    xs = jnp.pad(x_sample, ((0, 0), (0, T - Ts), (0, 0))).reshape(DB * T, D)
    bias_col = jnp.repeat(sb_bias, T).reshape(SB_HEADS * T, 1)
    head_eye = jnp.eye(SB_HEADS, dtype=F32)

    def sb_sample_fn(q_sb, k_sb, v_sb):
        q = q_sb.reshape(DB, T, SB_HEADS, SB_DH)
        qbd = (q.transpose(0, 2, 1, 3)[:, :, :, None, :] * head_eye[None, :, None, :, None]).reshape(
            DB, SB_HEADS * T, SB_W)
        out = _sb_sample(qbd, bias_col, k_sb.reshape(DB, T, SB_W), v_sb.reshape(DB, T, SB_W),
                         cache_sb_k, cache_sb_v, page_table)
        return out.reshape(DB * T, SB_W)

    ck = cache_swa_k.reshape(DB, WINDOW, SW_KV)
    cv = cache_swa_v.reshape(DB, WINDOW, SW_KV)
    y_s, (c_s, n_s, m_s, ksb_s, vsb_s, sre_s, sim_s, po_s) = _trunk(
        xs, (state_mlstm_C, state_mlstm_n, state_mlstm_m), (state_s5_re, state_s5_im),
        sb_sample_fn, functools.partial(_swa_sample, sinks=sinks, cache_k=ck, cache_v=cv, n_seq=DB, T=T),
        p, n_seq=DB, T=T, t_real=Ts)
    y_s = y_s.reshape(DB, T, D)[:, :Ts]
    sb_k_s, sb_v_s = ksb_s[:, :Ts], vsb_s[:, :Ts]
    po_s3 = po_s.reshape(DB, T, -1)[:, :Ts]
    k_new = po_s3[:, :, S5_CH + SW_Q:S5_CH + SW_Q + SW_KV].reshape(DB, Ts, SW_KV_HEADS, SW_DH)
    v_new = po_s3[:, :, S5_CH + SW_Q + SW_KV:].reshape(DB, Ts, SW_KV_HEADS, SW_DH)
    swa_k_s = jnp.concatenate([cache_swa_k[:, Ts:], k_new], axis=1)
    swa_v_s = jnp.concatenate([cache_swa_v[:, Ts:], v_new], axis=1)

    return (y_p.reshape(Bp, S, D), y_s, c_p, n_p, m_p, sb_k_p, sb_v_p, sre_p, sim_p, swa_k_p, swa_v_p,
            c_s, n_s, m_s, sb_k_s, sb_v_s, sre_s, sim_s, swa_k_s, swa_v_s)
```
